```python
import jax, jax.numpy as jnp
from jax import lax
import numpy as np

D_MODEL = 1024
BATCH = 2
SEQ = 8192
DEPTH = 1
DEC_BATCH = 128
DEC_SEQ = 1
PAST_LEN = 16384
PAGE_SIZE = 128

PLE_DIM = 256
ATT_HEADS = 8
ATT_KV_HEADS = 2
ATT_HEAD_DIM = 64
ATT_GROUP = ATT_HEADS // ATT_KV_HEADS
WINDOW = 128
ATT_BLOCK = 128
RET_HEADS = 4
RET_DK = 128
RET_DV = 128
RET_CHUNK = 128
ROPE_THETA = 10000.0
N_EXPERTS = 32
TOP_K = 4
D_FF = 1024
SWIGLU_LIMIT = 7.0
SWIGLU_ALPHA = 1.702
NORM_EPS = 1e-5
GN_EPS = 1e-6
NEG_INF = -1e30

ATT_Q_W = ATT_HEADS * ATT_HEAD_DIM
ATT_KV_W = ATT_KV_HEADS * ATT_HEAD_DIM
RET_QK_W = RET_HEADS * RET_DK
RET_V_W = RET_HEADS * RET_DV
MIX_W = ATT_Q_W + RET_V_W
IN_W = ATT_Q_W + 2 * ATT_KV_W + 2 * RET_QK_W + 2 * RET_V_W
IN_SPLITS = (ATT_Q_W,
             ATT_Q_W + ATT_KV_W,
             ATT_Q_W + 2 * ATT_KV_W,
             ATT_Q_W + 2 * ATT_KV_W + RET_QK_W,
             ATT_Q_W + 2 * ATT_KV_W + 2 * RET_QK_W,
             ATT_Q_W + 2 * ATT_KV_W + 2 * RET_QK_W + RET_V_W)

kernel_name = 'hybrid_swa_sink_retention_moe_ple_step'


def rms_norm(x, g):
    xf = x.astype(jnp.float32)
    y = xf * lax.rsqrt(jnp.mean(xf * xf, axis=-1, keepdims=True) + NORM_EPS)
    return (y * g.astype(jnp.float32)).astype(x.dtype)


def rope(x, pos):
    half = x.shape[-1] // 2
    inv = ROPE_THETA ** (-jnp.arange(half, dtype=jnp.float32) / half)
    ang = pos.astype(jnp.float32)[:, None] * inv[None, :]
    cos = jnp.cos(ang)[None, :, None, :]
    sin = jnp.sin(ang)[None, :, None, :]
    x1 = x[..., :half].astype(jnp.float32)
    x2 = x[..., half:].astype(jnp.float32)
    return jnp.concatenate([x1 * cos - x2 * sin, x2 * cos + x1 * sin], axis=-1).astype(x.dtype)


def mix_inputs(h, pos, w_in, b_in):
    B, L = h.shape[0], h.shape[1]
    z = h @ w_in + b_in
    aq, ak, av, rq, rk, rv, rg = jnp.split(z, IN_SPLITS, axis=-1)
    aq = rope(aq.reshape(B, L, ATT_HEADS, ATT_HEAD_DIM), pos)
    ak = rope(ak.reshape(B, L, ATT_KV_HEADS, ATT_HEAD_DIM), pos)
    av = av.reshape(B, L, ATT_KV_HEADS, ATT_HEAD_DIM)
    rq = rope(rq.reshape(B, L, RET_HEADS, RET_DK), pos)
    rk = rope(rk.reshape(B, L, RET_HEADS, RET_DK), pos) * (RET_DK ** -0.5)
    rv = rv.reshape(B, L, RET_HEADS, RET_DV)
    return aq, ak, av, rq, rk, rv, rg


def sink_softmax(s, sink):
    sink = sink.astype(jnp.float32)
    m = jnp.maximum(jnp.max(s, axis=-1, keepdims=True), sink)
    e = jnp.exp(s - m)
    return e / (jnp.sum(e, axis=-1, keepdims=True) + jnp.exp(sink - m))


def swa_prompt(q, k, v, sinks):
    B, S = q.shape[0], q.shape[1]
    NB = S // ATT_BLOCK
    qb = q.reshape(B, NB, ATT_BLOCK, ATT_KV_HEADS, ATT_GROUP, ATT_HEAD_DIM)
    kb = k.reshape(B, NB, ATT_BLOCK, ATT_KV_HEADS, ATT_HEAD_DIM)
    vb = v.reshape(B, NB, ATT_BLOCK, ATT_KV_HEADS, ATT_HEAD_DIM)
    pad = ((0, 0), (1, 0), (0, 0), (0, 0), (0, 0))
    kk = jnp.concatenate([jnp.pad(kb, pad)[:, :NB], kb], axis=2)
    vv = jnp.concatenate([jnp.pad(vb, pad)[:, :NB], vb], axis=2)
    s = jnp.einsum('bnqhgd,bnkhd->bnhgqk', qb, kk).astype(jnp.float32) * (ATT_HEAD_DIM ** -0.5)
    n = jnp.arange(NB)[:, None, None]
    i = jnp.arange(ATT_BLOCK)[None, :, None]
    j = jnp.arange(2 * ATT_BLOCK)[None, None, :]
    diff = i + ATT_BLOCK - j
    kpos = (n - 1) * ATT_BLOCK + j
    mask = (diff >= 0) & (diff < WINDOW) & (kpos >= 0)
    s = jnp.where(mask[None, :, None, None, :, :], s, NEG_INF)
    p = sink_softmax(s, sinks.reshape(ATT_KV_HEADS, ATT_GROUP)[:, :, None, None])
    o = jnp.einsum('bnhgqk,bnkhd->bnqhgd', p.astype(v.dtype), vv)
    return o.reshape(B, S, ATT_Q_W)


def swa_sample(q, k, v, buf_k, buf_v, sinks):
    DB, L = q.shape[0], q.shape[1]
    W_buf = buf_k.shape[1]
    kk = jnp.concatenate([buf_k, k], axis=1)
    vv = jnp.concatenate([buf_v, v], axis=1)
    qg = q.reshape(DB, L, ATT_KV_HEADS, ATT_GROUP, ATT_HEAD_DIM)
    s = jnp.einsum('bqhgd,bkhd->bhgqk', qg, kk).astype(jnp.float32) * (ATT_HEAD_DIM ** -0.5)
    diff = (jnp.arange(L)[:, None] + W_buf) - jnp.arange(W_buf + L)[None, :]
    mask = (diff >= 0) & (diff < WINDOW)
    s = jnp.where(mask[None, None, None, :, :], s, NEG_INF)
    p = sink_softmax(s, sinks.reshape(ATT_KV_HEADS, ATT_GROUP)[:, :, None, None])
    o = jnp.einsum('bhgqk,bkhd->bqhgd', p.astype(v.dtype), vv).reshape(DB, L, ATT_Q_W)
    return o, kk[:, L:], vv[:, L:]


def retention_log_decay():
    return jnp.log(1.0 - 2.0 ** (-5.0 - jnp.arange(RET_HEADS, dtype=jnp.float32)))


def retention_chunk(st, q, k, v, log_g):
    L = q.shape[1]
    dt = q.dtype
    t = jnp.arange(L, dtype=jnp.float32)
    rel = t[:, None] - t[None, :]
    intra = jnp.where(rel >= 0, jnp.exp(jnp.maximum(rel, 0.0)[None] * log_g[:, None, None]), 0.0)
    s = jnp.einsum('bihd,bjhd->bhij', q, k) * intra.astype(dt)
    o_intra = jnp.einsum('bhij,bjhe->bihe', s, v)
    cross = jnp.exp((t + 1.0)[:, None] * log_g[None, :]).astype(dt)
    o_cross = jnp.einsum('bihd,bhde->bihe', q, st.astype(dt)) * cross[None, :, :, None]
    kdec = jnp.exp((L - 1.0 - t)[:, None] * log_g[None, :]).astype(dt)
    st_new = (st * jnp.exp(L * log_g).astype(st.dtype)[None, :, None, None]
              + jnp.einsum('bjhd,bjhe->bhde', k * kdec[None, :, :, None], v).astype(st.dtype))
    return st_new, o_intra + o_cross


def retention_prompt(q, k, v, log_g):
    B, S = q.shape[0], q.shape[1]
    NC = S // RET_CHUNK

    def to_chunks(a):
        return jnp.moveaxis(a.reshape(B, NC, RET_CHUNK, a.shape[2], a.shape[3]), 1, 0)

    st0 = jnp.zeros((B, RET_HEADS, RET_DK, RET_DV), q.dtype)

    def step(st, xs):
        qc, kc, vc = xs
        return retention_chunk(st, qc, kc, vc, log_g)

    st_fin, o = lax.scan(step, st0, (to_chunks(q), to_chunks(k), to_chunks(v)))
    o = jnp.moveaxis(o, 0, 1).reshape(B, S, RET_HEADS, RET_DV)
    return o, st_fin


def merge_heads(att, ret, rg, g_ret_norm, w_out, b_out):
    B, L = att.shape[0], att.shape[1]
    of = ret.astype(jnp.float32)
    mu = jnp.mean(of, axis=-1, keepdims=True)
    var = jnp.mean(jnp.square(of - mu), axis=-1, keepdims=True)
    y = ((of - mu) * lax.rsqrt(var + GN_EPS)).reshape(B, L, RET_V_W) * g_ret_norm.astype(jnp.float32)
    ret_out = y.astype(ret.dtype) * jax.nn.silu(rg)
    return jnp.concatenate([att, ret_out], axis=-1) @ w_out + b_out


def moe(h, w_router, b_router, w_e_in, b_e_in, w_e_out, b_e_out):
    shape = h.shape
    t = h.reshape(-1, D_MODEL)
    logits = (t @ w_router + b_router).astype(jnp.float32)
    top_v, top_i = lax.top_k(logits, TOP_K)
    gates = jax.nn.softmax(top_v, axis=-1)
    combine = jnp.sum(jax.nn.one_hot(top_i, N_EXPERTS, dtype=jnp.float32) * gates[..., None], axis=1).astype(t.dtype)

    def expert(acc, xs):
        wi, bi, wo, bo, c = xs
        hg = t @ wi + bi
        gate = jnp.minimum(hg[:, :D_FF], SWIGLU_LIMIT)
        lin = jnp.clip(hg[:, D_FF:], -SWIGLU_LIMIT, SWIGLU_LIMIT)
        act = gate * jax.nn.sigmoid(SWIGLU_ALPHA * gate) * (lin + 1.0)
        return acc + c[:, None] * (act @ wo + bo), None

    acc, _ = lax.scan(expert, jnp.zeros_like(t), (w_e_in, b_e_in, w_e_out, b_e_out, combine.T))
    return acc.reshape(shape)


def channel_and_ple(x, p, g_mlp, w_router, b_router, w_e_in, b_e_in, w_e_out, b_e_out, g_ple, w_ple_gate, w_ple):
    x = x + moe(rms_norm(x, g_mlp), w_router, b_router, w_e_in, b_e_in, w_e_out, b_e_out)
    gate = jax.nn.sigmoid(rms_norm(x, g_ple) @ w_ple_gate)
    return x + (p @ w_ple) * gate


def setup_inputs(seed: int = 0) -> dict:
    key = jax.random.key(seed)
    ks = jax.random.split(key, 24)

    def nrm(k, shape, scale):
        return jax.random.normal(k, shape, jnp.float32) * scale

    buf = min(WINDOW, PAST_LEN)
    return {
        'x_prompt': nrm(ks[0], (BATCH, SEQ, D_MODEL), 1.0),
        'x_sample': nrm(ks[1], (DEC_BATCH, DEC_SEQ, D_MODEL), 1.0),
        'state_swa_k': nrm(ks[2], (DEPTH, DEC_BATCH, buf, ATT_KV_HEADS, ATT_HEAD_DIM), 1.0),
        'state_swa_v': nrm(ks[3], (DEPTH, DEC_BATCH, buf, ATT_KV_HEADS, ATT_HEAD_DIM), 1.0),
        'state_ret': nrm(ks[4], (DEPTH, DEC_BATCH, RET_HEADS, RET_DK, RET_DV), 0.5),
        'p_prompt': nrm(ks[5], (DEPTH, BATCH, SEQ, PLE_DIM), 1.0),
        'p_sample': nrm(ks[6], (DEPTH, DEC_BATCH, DEC_SEQ, PLE_DIM), 1.0),
        'g_mix': 1.0 + nrm(ks[7], (DEPTH, D_MODEL), 0.01),
        'w_in': nrm(ks[8], (DEPTH, D_MODEL, IN_W), D_MODEL ** -0.5),
        'b_in': nrm(ks[9], (DEPTH, IN_W), 0.01),
        'attn_sinks': nrm(ks[10], (DEPTH, ATT_HEADS), 1.0),
        'g_ret_norm': 1.0 + nrm(ks[11], (DEPTH, RET_V_W), 0.01),
        'w_out': nrm(ks[12], (DEPTH, MIX_W, D_MODEL), MIX_W ** -0.5),
        'b_out': nrm(ks[13], (DEPTH, D_MODEL), 0.01),
        'g_mlp': 1.0 + nrm(ks[14], (DEPTH, D_MODEL), 0.01),
        'w_router': nrm(ks[15], (DEPTH, D_MODEL, N_EXPERTS), D_MODEL ** -0.5),
        'b_router': nrm(ks[16], (DEPTH, N_EXPERTS), 0.01),
        'w_e_in': nrm(ks[17], (DEPTH, N_EXPERTS, D_MODEL, 2 * D_FF), D_MODEL ** -0.5),
        'b_e_in': nrm(ks[18], (DEPTH, N_EXPERTS, 2 * D_FF), 0.01),
        'w_e_out': nrm(ks[19], (DEPTH, N_EXPERTS, D_FF, D_MODEL), D_FF ** -0.5),
        'b_e_out': nrm(ks[20], (DEPTH, N_EXPERTS, D_MODEL), 0.01),
        'g_ple': 1.0 + nrm(ks[21], (DEPTH, D_MODEL), 0.01),
        'w_ple_gate': nrm(ks[22], (DEPTH, D_MODEL, D_MODEL), D_MODEL ** -0.5),
        'w_ple': nrm(ks[23], (DEPTH, PLE_DIM, D_MODEL), PLE_DIM ** -0.5),
        'g_final': 1.0 + nrm(jax.random.fold_in(key, 99), (D_MODEL,), 0.01),
    }


def reference(x_prompt, x_sample, state_swa_k, state_swa_v, state_ret, p_prompt, p_sample,
              g_mix, w_in, b_in, attn_sinks, g_ret_norm, w_out, b_out, g_mlp, w_router, b_router,
              w_e_in, b_e_in, w_e_out, b_e_out, g_ple, w_ple_gate, w_ple, g_final):
    pos_p = jnp.arange(x_prompt.shape[1], dtype=jnp.int32)
    pos_s = PAST_LEN + jnp.arange(x_sample.shape[1], dtype=jnp.int32)
    log_g = retention_log_decay()
    xp = x_prompt
    xs = x_sample
    kp_l, vp_l, sp_l, ks_l, vs_l, ss_l = [], [], [], [], [], []
    for l in range(DEPTH):
        h = rms_norm(xp, g_mix[l])
        aq, ak, av, rq, rk, rv, rg = mix_inputs(h, pos_p, w_in[l], b_in[l])
        att = swa_prompt(aq, ak, av, attn_sinks[l])
        ret, st_p = retention_prompt(rq, rk, rv, log_g)
        xp = xp + merge_heads(att, ret, rg, g_ret_norm[l], w_out[l], b_out[l])
        xp = channel_and_ple(xp, p_prompt[l], g_mlp[l], w_router[l], b_router[l], w_e_in[l], b_e_in[l],
                             w_e_out[l], b_e_out[l], g_ple[l], w_ple_gate[l], w_ple[l])
        kp_l.append(ak[:, -WINDOW:])
        vp_l.append(av[:, -WINDOW:])
        sp_l.append(st_p)
        h = rms_norm(xs, g_mix[l])
        aq, ak, av, rq, rk, rv, rg = mix_inputs(h, pos_s, w_in[l], b_in[l])
        att, kbuf, vbuf = swa_sample(aq, ak, av, state_swa_k[l], state_swa_v[l], attn_sinks[l])
        st_s, ret = retention_chunk(state_ret[l], rq, rk, rv, log_g)
        xs = xs + merge_heads(att, ret, rg, g_ret_norm[l], w_out[l], b_out[l])
        xs = channel_and_ple(xs, p_sample[l], g_mlp[l], w_router[l], b_router[l], w_e_in[l], b_e_in[l],
                             w_e_out[l], b_e_out[l], g_ple[l], w_ple_gate[l], w_ple[l])
        ks_l.append(kbuf)
        vs_l.append(vbuf)
        ss_l.append(st_s)
    y_prompt = rms_norm(xp, g_final)
    y_sample = rms_norm(xs, g_final)
    return (y_prompt, y_sample, jnp.stack(kp_l), jnp.stack(vp_l), jnp.stack(sp_l),
            jnp.stack(ks_l), jnp.stack(vs_l), jnp.stack(ss_l))
```

```python
import functools

import jax
import jax.numpy as jnp
from jax import lax
from jax.experimental import pallas as pl
from jax.experimental.pallas import tpu as pltpu

D_MODEL = 1024
PAST_LEN = 16384
PLE_DIM = 256
ATT_HEADS = 8
ATT_KV_HEADS = 2
ATT_HEAD_DIM = 64
ATT_GROUP = ATT_HEADS // ATT_KV_HEADS
WINDOW = 128
BLK = 128
RET_HEADS = 4
RET_DK = 128
RET_DV = 128
ROPE_THETA = 10000.0
N_EXPERTS = 32
TOP_K = 4
D_FF = 1024
SWIGLU_LIMIT = 7.0
SWIGLU_ALPHA = 1.702
NORM_EPS = 1e-5
GN_EPS = 1e-6
NEG_INF = -1e30

ATT_Q_W = ATT_HEADS * ATT_HEAD_DIM
ATT_KV_W = ATT_KV_HEADS * ATT_HEAD_DIM
RET_W = RET_HEADS * RET_DK
IN_W = ATT_Q_W + 2 * ATT_KV_W + 4 * RET_W
O_AQ = 0
O_AK = ATT_Q_W
O_AV = O_AK + ATT_KV_W
O_RQ = O_AV + ATT_KV_W
O_RK = O_RQ + RET_W
O_RV = O_RK + RET_W
O_RG = O_RV + RET_W

LANES = 128
META_IDX, META_RANK, META_GATE = 0, 4, 8

MIX_TOKENS = 256
DISPATCH_TOKENS = 256
EXPERT_ROWS = 256
COMBINE_TOKENS = 256
SAMPLE_CHUNK = 16
VMEM_LIMIT = 56 * 1024 * 1024

F32 = jnp.float32
BF16 = jnp.bfloat16


def _dot(a, b):
    return jnp.dot(a, b, preferred_element_type=F32)


def _dot_nt(a, b):
    return lax.dot_general(a, b, (((1,), (1,)), ((), ())), preferred_element_type=F32)


def _dot_tn(a, b):
    return lax.dot_general(a, b, (((0,), (0,)), ((), ())), preferred_element_type=F32)


def _rms(x, g):
    return x * lax.rsqrt(jnp.mean(x * x, axis=-1, keepdims=True) + NORM_EPS) * g


def _rope(col, cos, sin_signed, half):
    lane = lax.broadcasted_iota(jnp.int32, col.shape, 1)
    fwd = pltpu.roll(col, LANES - half, 1)
    bwd = pltpu.roll(col, half, 1)
    swapped = jnp.where((lane % (2 * half)) < half, fwd, bwd)
    return col * cos + swapped * sin_signed


def _in_proj(x, g_mix, w_in, b_in, cos64, sin64, cos128, sin128):
    h = _rms(x, g_mix).astype(BF16)
    z = _dot(h, w_in) + b_in

    def cols(off, width):
        return [z[:, off + c * LANES: off + (c + 1) * LANES] for c in range(width // LANES)]

    scale_q = ATT_HEAD_DIM ** -0.5
    aq = [_rope(c, cos64, sin64, ATT_HEAD_DIM // 2) * scale_q for c in cols(O_AQ, ATT_Q_W)]
    ak = _rope(z[:, O_AK:O_AK + LANES], cos64, sin64, ATT_HEAD_DIM // 2)
    av = z[:, O_AV:O_AV + LANES]
    rq = [_rope(c, cos128, sin128, RET_DK // 2) for c in cols(O_RQ, RET_W)]
    rk = [_rope(c, cos128, sin128, RET_DK // 2) * (RET_DK ** -0.5) for c in cols(O_RK, RET_W)]
    rv = cols(O_RV, RET_W)
    rg = z[:, O_RG:O_RG + RET_W]
    return aq, ak, av, rq, rk, rv, rg


def _expand_q(aq_cols, rows):
    lane = lax.broadcasted_iota(jnp.int32, (rows, LANES), 1)
    low = lane < ATT_HEAD_DIM
    pieces = []
    for h in range(ATT_HEADS):
        col, half, kv = aq_cols[h // 2], h % 2, h // ATT_GROUP
        if half != kv:
            col = pltpu.roll(col, ATT_HEAD_DIM, 1)
        pieces.append(jnp.where(low if kv == 0 else jnp.logical_not(low), col, 0.0))
    return jnp.concatenate(pieces, axis=0)


def _collect_heads(o, rows):
    lane = lax.broadcasted_iota(jnp.int32, (rows, LANES), 1)
    low = lane < ATT_HEAD_DIM
    out = []
    for c in range(ATT_HEADS // 2):
        pieces = []
        for half in (0, 1):
            h = 2 * c + half
            oh = o[h * rows:(h + 1) * rows]
            if h // ATT_GROUP != half:
                oh = pltpu.roll(oh, ATT_HEAD_DIM, 1)
            pieces.append(oh)
        out.append(jnp.where(low, pieces[0], pieces[1]))
    return jnp.concatenate(out, axis=1)


def _sink_softmax(s, mask, sink):
    s = jnp.where(mask, s, NEG_INF)
    m = jnp.maximum(jnp.max(s, axis=-1, keepdims=True), sink)
    e = jnp.exp(s - m)
    den = jnp.sum(e, axis=-1, keepdims=True) + jnp.exp(sink - m)
    return e * (1.0 / den)


def _group_norm_gate(ret, rg, g_ret):
    cols = []
    for h in range(RET_HEADS):
        of = ret[:, h * RET_DV:(h + 1) * RET_DV]
        d = of - jnp.mean(of, axis=-1, keepdims=True)
        var = jnp.mean(d * d, axis=-1, keepdims=True)
        cols.append(d * lax.rsqrt(var + GN_EPS))
    y = jnp.concatenate(cols, axis=1) * g_ret
    return y * (rg * jax.nn.sigmoid(rg))


def _route(x1, g_mlp, wr_hi, wr_lo, b_router, cnt):
    n = x1.shape[0]
    h2 = _rms(x1, g_mlp)
    hi = h2.astype(BF16)
    lo = (h2 - hi.astype(F32)).astype(BF16)
    logits = _dot(hi, wr_hi) + (_dot(hi, wr_lo) + _dot(lo, wr_hi)) + b_router
    lane = lax.broadcasted_iota(jnp.int32, (n, LANES), 1)
    work = jnp.where(lane < N_EXPERTS, logits, -jnp.inf)
    idxs, vals = [], []
    for _ in range(TOP_K):
        m = jnp.max(work, axis=-1, keepdims=True)
        idx = jnp.min(jnp.where(work == m, lane, LANES), axis=-1, keepdims=True)
        idxs.append(idx)
        vals.append(m)
        work = jnp.where(lane == idx, -jnp.inf, work)
    es = [jnp.exp(v - vals[0]) for v in vals]
    den = es[0] + es[1] + es[2] + es[3]
    gates = [e / den for e in es]

    sel = (lane == idxs[0]) | (lane == idxs[1]) | (lane == idxs[2]) | (lane == idxs[3])
    onehot = jnp.where(sel, 1.0, 0.0)
    r = lax.broadcasted_iota(jnp.int32, (n, n), 0)
    c = lax.broadcasted_iota(jnp.int32, (n, n), 1)
    below = jnp.where(r > c, 1.0, 0.0).astype(BF16)
    before = _dot(below, onehot.astype(BF16)) + cnt
    meta = jnp.zeros((n, LANES), F32)
    for k in range(TOP_K):
        rank = jnp.sum(jnp.where(lane == idxs[k], before, 0.0), axis=-1, keepdims=True)
        meta = jnp.where(lane == META_IDX + k, idxs[k].astype(F32), meta)
        meta = jnp.where(lane == META_RANK + k, rank, meta)
        meta = jnp.where(lane == META_GATE + k, gates[k], meta)
    return h2, meta, cnt + jnp.sum(onehot, axis=0, keepdims=True)


def _mixer_prompt_kernel(scal_ref, x_ref, cos64_ref, sin64_ref, cos128_ref, sin128_ref,
                         intra_ref, cross_ref, kdec_ref,
                         g_mix_ref, w_in_ref, b_in_ref, g_ret_ref, w_out_ref, b_out_ref,
                         g_mlp_ref, wr_hi_ref, wr_lo_ref, b_router_ref,
                         x1_ref, h2_ref, meta_ref, cnt_out_ref, kp_ref, vp_ref, stp_ref,
                         prevk, prevv, st, cnt):
    b, n = pl.program_id(0), pl.program_id(1)

    @pl.when((b == 0) & (n == 0))
    def _():
        cnt[...] = jnp.zeros_like(cnt)

    @pl.when(n == 0)
    def _():
        prevk[...] = jnp.zeros_like(prevk)
        prevv[...] = jnp.zeros_like(prevv)
        st[...] = jnp.zeros_like(st)

    x = x_ref[...]
    aq, ak, av, rq, rk, rv, rg = _in_proj(
        x, g_mix_ref[...], w_in_ref[...], b_in_ref[...],
        cos64_ref[...], sin64_ref[...], cos128_ref[...], sin128_ref[...])

    row = lax.broadcasted_iota(jnp.int32, (BLK, 2 * BLK), 0)
    col = lax.broadcasted_iota(jnp.int32, (BLK, 2 * BLK), 1)
    mask_prev = (col < BLK) & (col > row)
    mask_cur = (col >= BLK) & (col - BLK <= row)

    k_prev, v_prev = prevk[...], prevv[...]
    mixed = []
    for j in range(MIX_TOKENS // BLK):
        sl = slice(j * BLK, (j + 1) * BLK)
        kk = jnp.concatenate([k_prev, ak[sl]], axis=0).astype(BF16)
        vv = jnp.concatenate([v_prev, av[sl]], axis=0).astype(BF16)
        qall = _expand_q([cq[sl] for cq in aq], BLK).astype(BF16)
        s = _dot_nt(qall, kk)
        mask = mask_cur | mask_prev if j > 0 else mask_cur | (mask_prev & (n > 0))
        probs = [_sink_softmax(s[h * BLK:(h + 1) * BLK], mask, scal_ref[h]) for h in range(ATT_HEADS)]
        o = _dot(jnp.concatenate(probs, axis=0).astype(BF16), vv)
        att = _collect_heads(o, BLK)
        k_prev, v_prev = ak[sl], av[sl]

        ret = []
        for h in range(RET_HEADS):
            qb, kh, vb = rq[h][sl].astype(BF16), rk[h][sl], rv[h][sl].astype(BF16)
            sc = _dot_nt(qb, kh.astype(BF16)) * intra_ref[h]
            sth = st[h]
            o_h = _dot(sc.astype(BF16), vb) + _dot(qb, sth.astype(BF16)) * cross_ref[h]
            st[h] = sth * scal_ref[ATT_HEADS + h] + _dot_tn((kh * kdec_ref[h]).astype(BF16), vb)
            ret.append(o_h)
        ret_out = _group_norm_gate(jnp.concatenate(ret, axis=1), rg[sl], g_ret_ref[...])
        mixed.append(jnp.concatenate([att, ret_out], axis=1))

    mixed = jnp.concatenate(mixed, axis=0).astype(BF16)
    x1 = x + _dot(mixed, w_out_ref[...]) + b_out_ref[...]
    x1_ref[...] = x1

    c = cnt[0:1, :]
    for j in range(MIX_TOKENS // BLK):
        sl = slice(j * BLK, (j + 1) * BLK)
        h2, meta, c = _route(x1[sl], g_mlp_ref[...], wr_hi_ref[...], wr_lo_ref[...], b_router_ref[...], c)
        h2_ref[sl, :] = h2
        meta_ref[sl, :] = meta
    cnt[...] = jnp.broadcast_to(c, cnt.shape)
    cnt_out_ref[...] = cnt[...]

    prevk[...] = k_prev
    prevv[...] = v_prev
    kp_ref[0] = k_prev
    vp_ref[0] = v_prev
    stp_ref[0] = st[...]


def _const_spec(shape):
    nd = len(shape)
    return pl.BlockSpec(shape, lambda *_: (0,) * nd)


def _mixer_prompt(scal, x, tabs, ret_tabs, wts):
    bsz, seq, _ = x.shape
    steps = seq // MIX_TOKENS
    x2 = x.reshape(bsz * seq, D_MODEL)
    tok = lambda w: pl.BlockSpec((MIX_TOKENS, w), lambda b, n: (b * steps + n, 0))
    pos = lambda w: pl.BlockSpec((MIX_TOKENS, w), lambda b, n: (n, 0))
    in_specs = ([pl.BlockSpec(memory_space=pltpu.SMEM), tok(D_MODEL)]
                + [pos(LANES)] * 4
                + [_const_spec(t.shape) for t in ret_tabs]
                + [_const_spec(w.shape) for w in wts])
    out_shape = (
        jax.ShapeDtypeStruct((bsz * seq, D_MODEL), F32),
        jax.ShapeDtypeStruct((bsz * seq, D_MODEL), F32),
        jax.ShapeDtypeStruct((bsz * seq, LANES), F32),
        jax.ShapeDtypeStruct((8, LANES), F32),
        jax.ShapeDtypeStruct((bsz, BLK, LANES), F32),
        jax.ShapeDtypeStruct((bsz, BLK, LANES), F32),
        jax.ShapeDtypeStruct((bsz, RET_HEADS, RET_DK, RET_DV), F32),
    )
    out_specs = (
        tok(D_MODEL), tok(D_MODEL), tok(LANES), _const_spec((8, LANES)),
        pl.BlockSpec((1, BLK, LANES), lambda b, n: (b, 0, 0)),
        pl.BlockSpec((1, BLK, LANES), lambda b, n: (b, 0, 0)),
        pl.BlockSpec((1, RET_HEADS, RET_DK, RET_DV), lambda b, n: (b, 0, 0, 0)),
    )
    return pl.pallas_call(
        _mixer_prompt_kernel,
        grid=(bsz, steps),
        in_specs=in_specs,
        out_specs=out_specs,
        out_shape=out_shape,
        scratch_shapes=[pltpu.VMEM((BLK, LANES), F32), pltpu.VMEM((BLK, LANES), F32),
                        pltpu.VMEM((RET_HEADS, RET_DK, RET_DV), F32), pltpu.VMEM((8, LANES), F32)],
        compiler_params=pltpu.CompilerParams(
            dimension_semantics=("arbitrary", "arbitrary"), vmem_limit_bytes=VMEM_LIMIT),
        name="mixer_prompt",
    )(scal, x2, *tabs, *ret_tabs, *wts)


def _sample_inproj_kernel(x_ref, cos64_ref, sin64_ref, cos128_ref, sin128_ref,
                          g_mix_ref, w_in_ref, b_in_ref,
                          aq_ref, ak_ref, av_ref, rq_ref, rk_ref, rv_ref, rg_ref):
    aq, ak, av, rq, rk, rv, rg = _in_proj(
        x_ref[...], g_mix_ref[...], w_in_ref[...], b_in_ref[...],
        cos64_ref[...], sin64_ref[...], cos128_ref[...], sin128_ref[...])
    aq_ref[...] = jnp.concatenate(aq, axis=1)
    ak_ref[...] = ak
    av_ref[...] = av
    rq_ref[...] = jnp.concatenate(rq, axis=1)
    rk_ref[...] = jnp.concatenate(rk, axis=1)
    rv_ref[...] = jnp.concatenate(rv, axis=1)
    rg_ref[...] = rg


def _sample_inproj(x, tabs, g_mix, w_in, b_in):
    n = x.shape[0]
    widths = (ATT_Q_W, LANES, LANES, RET_W, RET_W, RET_W, RET_W)
    args = (x, *tabs, g_mix, w_in, b_in)
    return pl.pallas_call(
        _sample_inproj_kernel,
        grid=(1,),
        in_specs=[_const_spec(a.shape) for a in args],
        out_specs=tuple(_const_spec((n, w)) for w in widths),
        out_shape=tuple(jax.ShapeDtypeStruct((n, w), F32) for w in widths),
        compiler_params=pltpu.CompilerParams(vmem_limit_bytes=VMEM_LIMIT),
        name="sample_inproj",
    )(*args)


def _sample_state_kernel(scal_ref, aq_ref, ak_ref, av_ref, rq_ref, rk_ref, rv_ref,
                         kbuf_ref, vbuf_ref, st_ref,
                         att_ref, ret_ref, kout_ref, vout_ref, stout_ref):
    nb = SAMPLE_CHUNK
    rows = nb * WINDOW

    def shift_in(buf_ref, new):
        flat = buf_ref[...].reshape(rows, LANES)
        rolled = pltpu.roll(flat, rows - 1, 0)
        rolled = rolled.reshape(nb, WINDOW, LANES)
        pos = lax.broadcasted_iota(jnp.int32, (nb, WINDOW, LANES), 1)
        return jnp.where(pos == WINDOW - 1, new[:, None, :], rolled)

    knew = shift_in(kbuf_ref, ak_ref[...])
    vnew = shift_in(vbuf_ref, av_ref[...])
    kout_ref[...] = knew
    vout_ref[...] = vnew

    aq = aq_ref[...]
    qall = _expand_q([aq[:, c * LANES:(c + 1) * LANES] for c in range(ATT_Q_W // LANES)], nb)
    s = _dot_nt(qall.astype(BF16), knew.reshape(rows, LANES).astype(BF16))
    r = lax.broadcasted_iota(jnp.int32, (nb, rows), 0)
    c = lax.broadcasted_iota(jnp.int32, (nb, rows), 1)
    own = (c // WINDOW) == r
    probs = [_sink_softmax(s[h * nb:(h + 1) * nb], own, scal_ref[h]) for h in range(ATT_HEADS)]
    o = _dot(jnp.concatenate(probs, axis=0).astype(BF16), vnew.reshape(rows, LANES).astype(BF16))
    att_ref[...] = _collect_heads(o, nb)

    qt = rq_ref[...].T
    kt = rk_ref[...].T
    rv = rv_ref[...]
    for bi in range(nb):
        for h in range(RET_HEADS):
            hs = slice(h * RET_DK, (h + 1) * RET_DK)
            s_new = st_ref[bi, h] * scal_ref[ATT_HEADS + h] + kt[hs, bi:bi + 1] * rv[bi:bi + 1, hs]
            stout_ref[bi, h] = s_new
            ret_ref[bi:bi + 1, hs] = jnp.sum(qt[hs, bi:bi + 1] * s_new, axis=0, keepdims=True)


def _sample_state(scal, aq, ak, av, rq, rk, rv, kbuf, vbuf, st):
    n = aq.shape[0]
    nb = SAMPLE_CHUNK
    row = lambda w: pl.BlockSpec((nb, w), lambda i: (i, 0))
    buf = pl.BlockSpec((nb, WINDOW, LANES), lambda i: (i, 0, 0))
    stt = pl.BlockSpec((nb, RET_HEADS, RET_DK, RET_DV), lambda i: (i, 0, 0, 0))
    return pl.pallas_call(
        _sample_state_kernel,
        grid=(n // nb,),
        in_specs=[pl.BlockSpec(memory_space=pltpu.SMEM), row(ATT_Q_W), row(LANES), row(LANES),
                  row(RET_W), row(RET_W), row(RET_W), buf, buf, stt],
        out_specs=(row(ATT_Q_W), row(RET_W), buf, buf, stt),
        out_shape=(jax.ShapeDtypeStruct((n, ATT_Q_W), F32), jax.ShapeDtypeStruct((n, RET_W), F32),
                   jax.ShapeDtypeStruct(kbuf.shape, F32), jax.ShapeDtypeStruct(vbuf.shape, F32),
                   jax.ShapeDtypeStruct(st.shape, F32)),
        compiler_params=pltpu.CompilerParams(
            dimension_semantics=("arbitrary",), vmem_limit_bytes=VMEM_LIMIT),
        name="sample_state",
    )(scal, aq, ak, av, rq, rk, rv, kbuf, vbuf, st)


def _sample_post_kernel(x_ref, att_ref, ret_ref, rg_ref, cnt_in_ref,
                        g_ret_ref, w_out_ref, b_out_ref, g_mlp_ref, wr_hi_ref, wr_lo_ref, b_router_ref,
                        x1_ref, h2_ref, meta_ref, cnt_out_ref):
    x = x_ref[...]
    ret_out = _group_norm_gate(ret_ref[...], rg_ref[...], g_ret_ref[...])
    mixed = jnp.concatenate([att_ref[...], ret_out], axis=1).astype(BF16)
    x1 = x + _dot(mixed, w_out_ref[...]) + b_out_ref[...]
    x1_ref[...] = x1
    h2, meta, c = _route(x1, g_mlp_ref[...], wr_hi_ref[...], wr_lo_ref[...], b_router_ref[...],
                         cnt_in_ref[0:1, :])
    h2_ref[...] = h2
    meta_ref[...] = meta
    cnt_out_ref[...] = jnp.broadcast_to(c, cnt_out_ref.shape)


def _sample_post(x, att, ret, rg, cnt, wts):
    n = x.shape[0]
    vm = [x, att, ret, rg, cnt, *wts]
    widths = (D_MODEL, D_MODEL, LANES)
    return pl.pallas_call(
        _sample_post_kernel,
        grid=(1,),
        in_specs=[_const_spec(a.shape) for a in vm],
        out_specs=tuple(_const_spec((n, w)) for w in widths) + (_const_spec((8, LANES)),),
        out_shape=tuple(jax.ShapeDtypeStruct((n, w), F32) for w in widths)
                  + (jax.ShapeDtypeStruct((8, LANES), F32),),
        compiler_params=pltpu.CompilerParams(vmem_limit_bytes=VMEM_LIMIT),
        name="sample_post",
    )(*vm)


def _row_copy(src, src_row, dst, dst_row, sem):
    return pltpu.make_async_copy(src.at[pl.ds(src_row, 1)], dst.at[pl.ds(dst_row, 1)], sem)


def _dispatch_kernel(prompt_steps, n_dec, pos_hbm, h2p_ref, h2s_ref, xs_hbm, pos_smem, sem_idx, sem_rows):
    i = pl.program_id(0)

    def scatter_rows(src_ref, n_tok, first_idx):
        n_idx = n_tok * TOP_K
        idx_copy = pltpu.make_async_copy(pos_hbm.at[pl.ds(first_idx, n_idx)],
                                         pos_smem.at[pl.ds(0, n_idx)], sem_idx)
        idx_copy.start()
        idx_copy.wait()

        def issue(t, carry):
            for k in range(TOP_K):
                _row_copy(src_ref, t, xs_hbm, pos_smem[t * TOP_K + k], sem_rows).start()
            return carry

        lax.fori_loop(0, n_tok, issue, 0)

        def drain(t, carry):
            for k in range(TOP_K):
                _row_copy(src_ref, t, xs_hbm, pos_smem[t * TOP_K + k], sem_rows).wait()
            return carry

        lax.fori_loop(0, n_tok, drain, 0)

    @pl.when(i < prompt_steps)
    def _():
        scatter_rows(h2p_ref, DISPATCH_TOKENS, i * (DISPATCH_TOKENS * TOP_K))

    @pl.when(i == prompt_steps)
    def _():
        scatter_rows(h2s_ref, n_dec, prompt_steps * DISPATCH_TOKENS * TOP_K)


def _dispatch(pos_flat, h2_p, h2_s):
    n_prompt, n_dec = h2_p.shape[0], h2_s.shape[0]
    prompt_steps = n_prompt // DISPATCH_TOKENS
    return pl.pallas_call(
        functools.partial(_dispatch_kernel, prompt_steps, n_dec),
        grid=(prompt_steps + 1,),
        in_specs=[pl.BlockSpec(memory_space=pl.ANY),
                  pl.BlockSpec((DISPATCH_TOKENS, D_MODEL), lambda i: (jnp.minimum(i, prompt_steps - 1), 0)),
                  _const_spec(h2_s.shape)],
        out_specs=pl.BlockSpec(memory_space=pl.ANY),
        out_shape=jax.ShapeDtypeStruct(((n_prompt + n_dec) * TOP_K, D_MODEL), F32),
        scratch_shapes=[pltpu.SMEM((DISPATCH_TOKENS * TOP_K,), jnp.int32),
                        pltpu.SemaphoreType.DMA, pltpu.SemaphoreType.DMA],
        compiler_params=pltpu.CompilerParams(
            dimension_semantics=("arbitrary",), vmem_limit_bytes=VMEM_LIMIT),
        name="dispatch",
    )(pos_flat, h2_p, h2_s)


def _experts_kernel(v_tile, v_expert, v_lo, v_hi, v_newtile, v_newexp, n_visits,
                    xs_ref, w_in_ref, b_in_ref, w_out_ref, b_out_ref, y_ref, w_in_bf, w_out_bf):
    v = pl.program_id(0)

    @pl.when(v < n_visits[0])
    def _():
        @pl.when(v_newexp[v] == 1)
        def _():
            w_in_bf[...] = w_in_ref[0].astype(BF16)
            w_out_bf[...] = w_out_ref[0].astype(BF16)

        x = xs_ref[...].astype(BF16)
        hg = _dot(x, w_in_bf[...]) + b_in_ref[0]
        gate = jnp.minimum(hg[:, :D_FF], SWIGLU_LIMIT)
        lin = jnp.clip(hg[:, D_FF:], -SWIGLU_LIMIT, SWIGLU_LIMIT)
        act = gate * jax.nn.sigmoid(SWIGLU_ALPHA * gate) * (lin + 1.0)
        y = _dot(act.astype(BF16), w_out_bf[...]) + b_out_ref[0]
        row = lax.broadcasted_iota(jnp.int32, y.shape, 0)
        mine = (row >= v_lo[v]) & (row < v_hi[v])

        @pl.when(v_newtile[v] == 1)
        def _():
            y_ref[...] = jnp.where(mine, y, 0.0)

        @pl.when(v_newtile[v] == 0)
        def _():
            y_ref[...] = jnp.where(mine, y, y_ref[...])


def _experts(visits, xs, w_e_in, b_e_in, w_e_out, b_e_out):
    n_rows = xs.shape[0]
    max_visits = n_rows // EXPERT_ROWS + N_EXPERTS - 1
    grid_spec = pltpu.PrefetchScalarGridSpec(
        num_scalar_prefetch=7,
        grid=(max_visits,),
        in_specs=[
            pl.BlockSpec((EXPERT_ROWS, D_MODEL), lambda v, vt, ve, *_: (vt[v], 0)),
            pl.BlockSpec((1, D_MODEL, 2 * D_FF), lambda v, vt, ve, *_: (ve[v], 0, 0)),
            pl.BlockSpec((1, 1, 2 * D_FF), lambda v, vt, ve, *_: (ve[v], 0, 0)),
            pl.BlockSpec((1, D_FF, D_MODEL), lambda v, vt, ve, *_: (ve[v], 0, 0)),
            pl.BlockSpec((1, 1, D_MODEL), lambda v, vt, ve, *_: (ve[v], 0, 0)),
        ],
        out_specs=pl.BlockSpec((EXPERT_ROWS, D_MODEL), lambda v, vt, ve, *_: (vt[v], 0)),
        scratch_shapes=[pltpu.VMEM((D_MODEL, 2 * D_FF), BF16), pltpu.VMEM((D_FF, D_MODEL), BF16)],
    )
    return pl.pallas_call(
        _experts_kernel,
        grid_spec=grid_spec,
        out_shape=jax.ShapeDtypeStruct((n_rows, D_MODEL), F32),
        compiler_params=pltpu.CompilerParams(
            dimension_semantics=("arbitrary",), vmem_limit_bytes=VMEM_LIMIT),
        name="experts",
    )(*visits, xs, w_e_in, b_e_in.reshape(N_EXPERTS, 1, 2 * D_FF), w_e_out,
      b_e_out.reshape(N_EXPERTS, 1, D_MODEL))


def _visit_schedule(counts, n_rows):
    n_tiles = n_rows // EXPERT_ROWS
    max_visits = n_tiles + N_EXPERTS - 1
    ends = jnp.cumsum(counts)
    starts = ends - counts
    first_tile = starts // EXPERT_ROWS
    tiles_e = jnp.where(counts > 0, (ends - 1) // EXPERT_ROWS - first_tile + 1, 0)
    vis_end = jnp.cumsum(tiles_e)
    vis_start = vis_end - tiles_e
    total = vis_end[-1]
    v = jnp.minimum(jnp.arange(max_visits, dtype=jnp.int32), total - 1)
    e = jnp.searchsorted(vis_end, v, side="right").astype(jnp.int32)
    tile = first_tile[e] + (v - vis_start[e])
    lo = jnp.maximum(starts[e], tile * EXPERT_ROWS) - tile * EXPERT_ROWS
    hi = jnp.minimum(ends[e], (tile + 1) * EXPERT_ROWS) - tile * EXPERT_ROWS
    prev_tile = jnp.concatenate([jnp.full((1,), -1, jnp.int32), tile[:-1]])
    prev_e = jnp.concatenate([jnp.full((1,), -1, jnp.int32), e[:-1]])
    as_i32 = lambda a: a.astype(jnp.int32)
    return (as_i32(tile), as_i32(e), as_i32(lo), as_i32(hi), as_i32(tile != prev_tile),
            as_i32(e != prev_e), as_i32(total).reshape(1))


def _combine_kernel(tokens, first_idx, pos_hbm, x1_ref, meta_ref, p_ref, ys_hbm,
                    g_ple_ref, w_pg_ref, w_ple_ref, g_final_ref, y_ref,
                    pos_smem, rows, sem_idx, sem_rows):
    i = pl.program_id(0)
    n_idx = tokens * TOP_K
    idx_copy = pltpu.make_async_copy(pos_hbm.at[pl.ds(first_idx + i * n_idx, n_idx)], pos_smem, sem_idx)
    idx_copy.start()
    idx_copy.wait()

    def issue(t, carry):
        for k in range(TOP_K):
            _row_copy(ys_hbm, pos_smem[t * TOP_K + k], rows.at[k], t, sem_rows).start()
        return carry

    lax.fori_loop(0, tokens, issue, 0)

    def drain(t, carry):
        for k in range(TOP_K):
            _row_copy(ys_hbm, pos_smem[t * TOP_K + k], rows.at[k], t, sem_rows).wait()
        return carry

    lax.fori_loop(0, tokens, drain, 0)

    meta = meta_ref[...]
    x2 = x1_ref[...]
    for k in range(TOP_K):
        x2 = x2 + meta[:, META_GATE + k:META_GATE + k + 1] * rows[k]
    hp = _rms(x2, g_ple_ref[...]).astype(BF16)
    gate = jax.nn.sigmoid(_dot(hp, w_pg_ref[...]))
    x3 = x2 + _dot(p_ref[...].astype(BF16), w_ple_ref[...]) * gate
    y_ref[...] = _rms(x3, g_final_ref[...])


def _combine(tokens, first_idx, pos_flat, x1, meta, p, ys, wts):
    n_out = x1.shape[0]
    tok = lambda w: pl.BlockSpec((tokens, w), lambda i: (i, 0))
    any_spec = pl.BlockSpec(memory_space=pl.ANY)
    return pl.pallas_call(
        functools.partial(_combine_kernel, tokens, first_idx),
        grid=(n_out // tokens,),
        in_specs=[any_spec, tok(D_MODEL), tok(LANES), tok(PLE_DIM), any_spec]
                 + [_const_spec(w.shape) for w in wts],
        out_specs=pl.BlockSpec((tokens, D_MODEL), lambda i: (i, 0)),
        out_shape=jax.ShapeDtypeStruct((n_out, D_MODEL), F32),
        scratch_shapes=[pltpu.SMEM((tokens * TOP_K,), jnp.int32),
                        pltpu.VMEM((TOP_K, tokens, D_MODEL), F32),
                        pltpu.SemaphoreType.DMA, pltpu.SemaphoreType.DMA],
        compiler_params=pltpu.CompilerParams(
            dimension_semantics=("arbitrary",), vmem_limit_bytes=VMEM_LIMIT),
        name="combine_ple",
    )(pos_flat, x1, meta, p, ys, *wts)


def _rope_tables(pos, n_rows):
    out = []
    for dim in (ATT_HEAD_DIM, RET_DK):
        half = dim // 2
        inv = ROPE_THETA ** (-jnp.arange(half, dtype=F32) / half)
        ang = pos.astype(F32)[:, None] * inv[None, :]
        cos, sin = jnp.cos(ang), jnp.sin(ang)
        reps = LANES // dim
        cos_row = jnp.tile(jnp.concatenate([cos, cos], axis=-1), (1, reps))
        sin_row = jnp.tile(jnp.concatenate([-sin, sin], axis=-1), (1, reps))
        out += [jnp.broadcast_to(cos_row, (n_rows, LANES)), jnp.broadcast_to(sin_row, (n_rows, LANES))]
    return out


def _retention_tables():
    log_g = jnp.log(1.0 - 2.0 ** (-5.0 - jnp.arange(RET_HEADS, dtype=F32)))
    t = jnp.arange(BLK, dtype=F32)
    rel = t[:, None] - t[None, :]
    intra = jnp.where(rel >= 0, jnp.exp(jnp.maximum(rel, 0.0)[None] * log_g[:, None, None]), 0.0)
    cross = jnp.exp((t + 1.0)[None, :] * log_g[:, None])
    kdec = jnp.exp((BLK - 1.0 - t)[None, :] * log_g[:, None])
    lane_bcast = lambda a: jnp.broadcast_to(a[:, :, None], (RET_HEADS, BLK, LANES))
    return log_g, (intra, lane_bcast(cross), lane_bcast(kdec))


def kernel(x_prompt, x_sample, state_swa_k, state_swa_v, state_ret, p_prompt, p_sample, g_mix, w_in, b_in,
           attn_sinks, g_ret_norm, w_out, b_out, g_mlp, w_router, b_router, w_e_in, b_e_in, w_e_out,
           b_e_out, g_ple, w_ple_gate, w_ple, g_final):
    bsz, seq, _ = x_prompt.shape
    n_dec = x_sample.shape[0]
    n_prompt = bsz * seq
    assert x_sample.shape[1] == 1 and g_mix.shape[0] == 1
    assert state_swa_k.shape[2] == WINDOW and n_dec <= DISPATCH_TOKENS

    row = lambda a: a.reshape(1, -1)
    log_g, ret_tabs = _retention_tables()
    sinks = attn_sinks[0].astype(F32)
    scal_prompt = jnp.concatenate([sinks, jnp.exp(BLK * log_g)])
    scal_sample = jnp.concatenate([sinks, jnp.exp(log_g)])

    w_in_bf = w_in[0].astype(BF16)
    w_out_bf = w_out[0].astype(BF16)
    wr = jnp.pad(w_router[0], ((0, 0), (0, LANES - N_EXPERTS)))
    wr_hi = wr.astype(BF16)
    wr_lo = (wr - wr_hi.astype(F32)).astype(BF16)
    br = jnp.pad(b_router[0], (0, LANES - N_EXPERTS)).reshape(1, LANES)
    post_wts = (row(g_ret_norm[0]), w_out_bf, row(b_out[0]), row(g_mlp[0]), wr_hi, wr_lo, br)

    tabs_p = _rope_tables(jnp.arange(seq, dtype=jnp.int32), seq)
    x1_p, h2_p, meta_p, cnt, kp, vp, st_p = _mixer_prompt(
        scal_prompt, x_prompt, tabs_p, ret_tabs, (row(g_mix[0]), w_in_bf, row(b_in[0])) + post_wts)

    tabs_s = _rope_tables(jnp.full((1,), PAST_LEN, jnp.int32), n_dec)
    xs2 = x_sample.reshape(n_dec, D_MODEL)
    aq, ak, av, rq, rk, rv, rg = _sample_inproj(xs2, tabs_s, row(g_mix[0]), w_in_bf, row(b_in[0]))
    att, ret, ks, vs, st_s = _sample_state(
        scal_sample, aq, ak, av, rq, rk, rv,
        state_swa_k[0].reshape(n_dec, WINDOW, LANES), state_swa_v[0].reshape(n_dec, WINDOW, LANES),
        state_ret[0])
    x1_s, h2_s, meta_s, cnt = _sample_post(xs2, att, ret, rg, cnt, post_wts)

    counts = cnt[0, :N_EXPERTS].astype(jnp.int32)
    starts = jnp.cumsum(counts) - counts
    routed = jnp.concatenate([meta_p[:, :META_GATE], meta_s[:, :META_GATE]], axis=0).astype(jnp.int32)
    sel, rank = routed[:, META_IDX:META_IDX + TOP_K], routed[:, META_RANK:META_RANK + TOP_K]
    pos_flat = (starts[sel] + rank).reshape(-1)

    xs_sorted = _dispatch(pos_flat, h2_p, h2_s)
    visits = _visit_schedule(counts, (n_prompt + n_dec) * TOP_K)
    ys_sorted = _experts(visits, xs_sorted, w_e_in[0], b_e_in[0], w_e_out[0], b_e_out[0])

    ple_wts = (row(g_ple[0]), w_ple_gate[0].astype(BF16), w_ple[0].astype(BF16), row(g_final))
    y_p = _combine(COMBINE_TOKENS, 0, pos_flat, x1_p, meta_p,
                   p_prompt[0].reshape(n_prompt, PLE_DIM), ys_sorted, ple_wts)
    y_s = _combine(n_dec, n_prompt * TOP_K, pos_flat, x1_s, meta_s,
                   p_sample[0].reshape(n_dec, PLE_DIM), ys_sorted, ple_wts)

    kv_shape = (1, bsz, WINDOW, ATT_KV_HEADS, ATT_HEAD_DIM)
    dec_shape = (1, n_dec, WINDOW, ATT_KV_HEADS, ATT_HEAD_DIM)
    return (y_p.reshape(bsz, seq, D_MODEL), y_s.reshape(n_dec, 1, D_MODEL),
            kp.reshape(kv_shape), vp.reshape(kv_shape), st_p[None],
            ks.reshape(dec_shape), vs.reshape(dec_shape), st_s[None])
```

```python
import functools

import jax
import jax.numpy as jnp
import numpy as np
from jax import lax
from jax.experimental import pallas as pl
from jax.experimental.pallas import tpu as pltpu

D_MODEL = 1024
PAST_LEN = 16384
PLE_DIM = 256
ATT_HEADS = 8
ATT_KV_HEADS = 2
ATT_HEAD_DIM = 64
ATT_GROUP = ATT_HEADS // ATT_KV_HEADS
WINDOW = 128
BLK = 128
RET_HEADS = 4
RET_DK = 128
RET_DV = 128
ROPE_THETA = 10000.0
N_EXPERTS = 32
TOP_K = 4
D_FF = 1024
SWIGLU_LIMIT = 7.0
SWIGLU_ALPHA = 1.702
NORM_EPS = 1e-5
GN_EPS = 1e-6
NEG_INF = -1e30

ATT_Q_W = ATT_HEADS * ATT_HEAD_DIM
ATT_KV_W = ATT_KV_HEADS * ATT_HEAD_DIM
RET_W = RET_HEADS * RET_DK
IN_W = ATT_Q_W + 2 * ATT_KV_W + 4 * RET_W
O_AQ = 0
O_AK = ATT_Q_W
O_AV = O_AK + ATT_KV_W
O_RQ = O_AV + ATT_KV_W
O_RK = O_RQ + RET_W
O_RV = O_RK + RET_W
O_RG = O_RV + RET_W

LANES = 128
META_IDX, META_RANK, META_GATE = 0, 4, 8

MIX_TOKENS = 256
DISPATCH_TOKENS = 256
EXPERT_ROWS = 256
COMBINE_TOKENS = 256
SAMPLE_CHUNK = 16
ROW_UNROLL = 4
VMEM_LIMIT = 56 * 1024 * 1024

F32 = jnp.float32
BF16 = jnp.bfloat16


def _dot(a, b):
    return jnp.dot(a, b, preferred_element_type=F32)


def _dot_nt(a, b):
    return lax.dot_general(a, b, (((1,), (1,)), ((), ())), preferred_element_type=F32)


def _dot_tn(a, b):
    return lax.dot_general(a, b, (((0,), (0,)), ((), ())), preferred_element_type=F32)


def _rms(x, g):
    return x * lax.rsqrt(jnp.mean(x * x, axis=-1, keepdims=True) + NORM_EPS) * g


def _rope(col, cos, sin_signed, half):
    lane = lax.broadcasted_iota(jnp.int32, col.shape, 1)
    fwd = pltpu.roll(col, LANES - half, 1)
    bwd = pltpu.roll(col, half, 1)
    swapped = jnp.where((lane % (2 * half)) < half, fwd, bwd)
    return col * cos + swapped * sin_signed


def _in_proj(x, g_mix, w_in, b_in, cos64, sin64, cos128, sin128):
    h = _rms(x, g_mix).astype(BF16)
    z = _dot(h, w_in) + b_in

    def cols(off, width):
        return [z[:, off + c * LANES: off + (c + 1) * LANES] for c in range(width // LANES)]

    scale_q = ATT_HEAD_DIM ** -0.5
    aq = [_rope(c, cos64, sin64, ATT_HEAD_DIM // 2) * scale_q for c in cols(O_AQ, ATT_Q_W)]
    ak = _rope(z[:, O_AK:O_AK + LANES], cos64, sin64, ATT_HEAD_DIM // 2)
    av = z[:, O_AV:O_AV + LANES]
    rq = [_rope(c, cos128, sin128, RET_DK // 2) for c in cols(O_RQ, RET_W)]
    rk = [_rope(c, cos128, sin128, RET_DK // 2) * (RET_DK ** -0.5) for c in cols(O_RK, RET_W)]
    rv = cols(O_RV, RET_W)
    rg = z[:, O_RG:O_RG + RET_W]
    return aq, ak, av, rq, rk, rv, rg


def _expand_q(aq_cols, rows):
    lane = lax.broadcasted_iota(jnp.int32, (rows, LANES), 1)
    low = lane < ATT_HEAD_DIM
    pieces = []
    for h in range(ATT_HEADS):
        col, half, kv = aq_cols[h // 2], h % 2, h // ATT_GROUP
        if half != kv:
            col = pltpu.roll(col, ATT_HEAD_DIM, 1)
        pieces.append(jnp.where(low if kv == 0 else jnp.logical_not(low), col, 0.0))
    return jnp.concatenate(pieces, axis=0)


def _collect_heads(o, rows):
    lane = lax.broadcasted_iota(jnp.int32, (rows, LANES), 1)
    low = lane < ATT_HEAD_DIM
    out = []
    for c in range(ATT_HEADS // 2):
        pieces = []
        for half in (0, 1):
            h = 2 * c + half
            oh = o[h * rows:(h + 1) * rows]
            if h // ATT_GROUP != half:
                oh = pltpu.roll(oh, ATT_HEAD_DIM, 1)
            pieces.append(oh)
        out.append(jnp.where(low, pieces[0], pieces[1]))
    return jnp.concatenate(out, axis=1)


def _sink_softmax(s, mask, sink):
    s = jnp.where(mask, s, NEG_INF)
    m = jnp.maximum(jnp.max(s, axis=-1, keepdims=True), sink)
    e = jnp.exp(s - m)
    den = jnp.sum(e, axis=-1, keepdims=True) + jnp.exp(sink - m)
    return e * (1.0 / den)


def _group_norm_gate(ret, rg, g_ret):
    cols = []
    for h in range(RET_HEADS):
        of = ret[:, h * RET_DV:(h + 1) * RET_DV]
        d = of - jnp.mean(of, axis=-1, keepdims=True)
        var = jnp.mean(d * d, axis=-1, keepdims=True)
        cols.append(d * lax.rsqrt(var + GN_EPS))
    y = jnp.concatenate(cols, axis=1) * g_ret
    return y * (rg * jax.nn.sigmoid(rg))


def _route(x1, g_mlp, wr_hi, wr_lo, b_router, cnt):
    n = x1.shape[0]
    h2 = _rms(x1, g_mlp)
    hi = h2.astype(BF16)
    lo = (h2 - hi.astype(F32)).astype(BF16)
    logits = _dot(hi, wr_hi) + (_dot(hi, wr_lo) + _dot(lo, wr_hi)) + b_router
    lane = lax.broadcasted_iota(jnp.int32, (n, LANES), 1)
    work = jnp.where(lane < N_EXPERTS, logits, -jnp.inf)
    idxs, vals = [], []
    for _ in range(TOP_K):
        m = jnp.max(work, axis=-1, keepdims=True)
        idx = jnp.min(jnp.where(work == m, lane, LANES), axis=-1, keepdims=True)
        idxs.append(idx)
        vals.append(m)
        work = jnp.where(lane == idx, -jnp.inf, work)
    es = [jnp.exp(v - vals[0]) for v in vals]
    den = es[0] + es[1] + es[2] + es[3]
    gates = [e / den for e in es]

    sel = (lane == idxs[0]) | (lane == idxs[1]) | (lane == idxs[2]) | (lane == idxs[3])
    onehot = jnp.where(sel, 1.0, 0.0)
    r = lax.broadcasted_iota(jnp.int32, (n, n), 0)
    c = lax.broadcasted_iota(jnp.int32, (n, n), 1)
    below = jnp.where(r > c, 1.0, 0.0).astype(BF16)
    before = _dot(below, onehot.astype(BF16)) + cnt
    meta = jnp.zeros((n, LANES), F32)
    for k in range(TOP_K):
        rank = jnp.sum(jnp.where(lane == idxs[k], before, 0.0), axis=-1, keepdims=True)
        meta = jnp.where(lane == META_IDX + k, idxs[k].astype(F32), meta)
        meta = jnp.where(lane == META_RANK + k, rank, meta)
        meta = jnp.where(lane == META_GATE + k, gates[k], meta)
    return h2, meta, cnt + jnp.sum(onehot, axis=0, keepdims=True)


def _mixer_prompt_kernel(scal_ref, x_ref, cos64_ref, sin64_ref, cos128_ref, sin128_ref,
                         intra_ref, cross_ref, kdec_ref,
                         g_mix_ref, w_in_ref, b_in_ref, g_ret_ref, w_out_ref, b_out_ref,
                         g_mlp_ref, wr_hi_ref, wr_lo_ref, b_router_ref,
                         x1_ref, h2_ref, meta_ref, cnt_out_ref, kp_ref, vp_ref, stp_ref,
                         prevk, prevv, st, cnt):
    b, n = pl.program_id(0), pl.program_id(1)

    @pl.when((b == 0) & (n == 0))
    def _():
        cnt[...] = jnp.zeros_like(cnt)

    @pl.when(n == 0)
    def _():
        prevk[...] = jnp.zeros_like(prevk)
        prevv[...] = jnp.zeros_like(prevv)
        st[...] = jnp.zeros_like(st)

    x = x_ref[...]
    aq, ak, av, rq, rk, rv, rg = _in_proj(
        x, g_mix_ref[...], w_in_ref[...], b_in_ref[...],
        cos64_ref[...], sin64_ref[...], cos128_ref[...], sin128_ref[...])

    row = lax.broadcasted_iota(jnp.int32, (BLK, 2 * BLK), 0)
    col = lax.broadcasted_iota(jnp.int32, (BLK, 2 * BLK), 1)
    mask_prev = (col < BLK) & (col > row)
    mask_cur = (col >= BLK) & (col - BLK <= row)

    k_prev, v_prev = prevk[...], prevv[...]
    mixed = []
    for j in range(MIX_TOKENS // BLK):
        sl = slice(j * BLK, (j + 1) * BLK)
        kk = jnp.concatenate([k_prev, ak[sl]], axis=0).astype(BF16)
        vv = jnp.concatenate([v_prev, av[sl]], axis=0).astype(BF16)
        qall = _expand_q([cq[sl] for cq in aq], BLK).astype(BF16)
        s = _dot_nt(qall, kk)
        mask = mask_cur | mask_prev if j > 0 else mask_cur | (mask_prev & (n > 0))
        probs = [_sink_softmax(s[h * BLK:(h + 1) * BLK], mask, scal_ref[h]) for h in range(ATT_HEADS)]
        o = _dot(jnp.concatenate(probs, axis=0).astype(BF16), vv)
        att = _collect_heads(o, BLK)
        k_prev, v_prev = ak[sl], av[sl]

        ret = []
        for h in range(RET_HEADS):
            qb, kh, vb = rq[h][sl].astype(BF16), rk[h][sl], rv[h][sl].astype(BF16)
            sc = _dot_nt(qb, kh.astype(BF16)) * intra_ref[h]
            sth = st[h]
            o_h = _dot(sc.astype(BF16), vb) + _dot(qb, sth.astype(BF16)) * cross_ref[h]
            st[h] = sth * scal_ref[ATT_HEADS + h] + _dot_tn((kh * kdec_ref[h]).astype(BF16), vb)
            ret.append(o_h)
        ret_out = _group_norm_gate(jnp.concatenate(ret, axis=1), rg[sl], g_ret_ref[...])
        mixed.append(jnp.concatenate([att, ret_out], axis=1))

    mixed = jnp.concatenate(mixed, axis=0).astype(BF16)
    x1 = x + _dot(mixed, w_out_ref[...]) + b_out_ref[...]
    x1_ref[...] = x1

    c = cnt[0:1, :]
    for j in range(MIX_TOKENS // BLK):
        sl = slice(j * BLK, (j + 1) * BLK)
        h2, meta, c = _route(x1[sl], g_mlp_ref[...], wr_hi_ref[...], wr_lo_ref[...], b_router_ref[...], c)
        h2_ref[sl, :] = h2
        meta_ref[sl, :] = meta
    cnt[...] = jnp.broadcast_to(c, cnt.shape)
    cnt_out_ref[...] = cnt[...]

    prevk[...] = k_prev
    prevv[...] = v_prev
    kp_ref[0] = k_prev
    vp_ref[0] = v_prev
    stp_ref[0] = st[...]


def _const_spec(shape):
    nd = len(shape)
    return pl.BlockSpec(shape, lambda *_: (0,) * nd)


def _mixer_prompt(scal, x, tabs, ret_tabs, wts):
    bsz, seq, _ = x.shape
    steps = seq // MIX_TOKENS
    x2 = x.reshape(bsz * seq, D_MODEL)
    tok = lambda w: pl.BlockSpec((MIX_TOKENS, w), lambda b, n: (b * steps + n, 0))
    pos = lambda w: pl.BlockSpec((MIX_TOKENS, w), lambda b, n: (n, 0))
    in_specs = ([pl.BlockSpec(memory_space=pltpu.SMEM), tok(D_MODEL)]
                + [pos(LANES)] * 4
                + [_const_spec(t.shape) for t in ret_tabs]
                + [_const_spec(w.shape) for w in wts])
    out_shape = (
        jax.ShapeDtypeStruct((bsz * seq, D_MODEL), F32),
        jax.ShapeDtypeStruct((bsz * seq, D_MODEL), F32),
        jax.ShapeDtypeStruct((bsz * seq, LANES), F32),
        jax.ShapeDtypeStruct((8, LANES), F32),
        jax.ShapeDtypeStruct((bsz, BLK, LANES), F32),
        jax.ShapeDtypeStruct((bsz, BLK, LANES), F32),
        jax.ShapeDtypeStruct((bsz, RET_HEADS, RET_DK, RET_DV), F32),
    )
    out_specs = (
        tok(D_MODEL), tok(D_MODEL), tok(LANES), _const_spec((8, LANES)),
        pl.BlockSpec((1, BLK, LANES), lambda b, n: (b, 0, 0)),
        pl.BlockSpec((1, BLK, LANES), lambda b, n: (b, 0, 0)),
        pl.BlockSpec((1, RET_HEADS, RET_DK, RET_DV), lambda b, n: (b, 0, 0, 0)),
    )
    return pl.pallas_call(
        _mixer_prompt_kernel,
        grid=(bsz, steps),
        in_specs=in_specs,
        out_specs=out_specs,
        out_shape=out_shape,
        scratch_shapes=[pltpu.VMEM((BLK, LANES), F32), pltpu.VMEM((BLK, LANES), F32),
                        pltpu.VMEM((RET_HEADS, RET_DK, RET_DV), F32), pltpu.VMEM((8, LANES), F32)],
        compiler_params=pltpu.CompilerParams(
            dimension_semantics=("arbitrary", "arbitrary"), vmem_limit_bytes=VMEM_LIMIT),
        name="mixer_prompt",
    )(scal, x2, *tabs, *ret_tabs, *wts)


def _sample_inproj_kernel(x_ref, cos64_ref, sin64_ref, cos128_ref, sin128_ref,
                          g_mix_ref, w_in_ref, b_in_ref,
                          aq_ref, ak_ref, av_ref, rq_ref, rk_ref, rv_ref, rg_ref):
    aq, ak, av, rq, rk, rv, rg = _in_proj(
        x_ref[...], g_mix_ref[...], w_in_ref[...], b_in_ref[...],
        cos64_ref[...], sin64_ref[...], cos128_ref[...], sin128_ref[...])
    aq_ref[...] = jnp.concatenate(aq, axis=1)
    ak_ref[...] = ak
    av_ref[...] = av
    rq_ref[...] = jnp.concatenate(rq, axis=1)
    rk_ref[...] = jnp.concatenate(rk, axis=1)
    rv_ref[...] = jnp.concatenate(rv, axis=1)
    rg_ref[...] = rg


def _sample_inproj(x, tabs, g_mix, w_in, b_in):
    n = x.shape[0]
    widths = (ATT_Q_W, LANES, LANES, RET_W, RET_W, RET_W, RET_W)
    args = (x, *tabs, g_mix, w_in, b_in)
    return pl.pallas_call(
        _sample_inproj_kernel,
        grid=(1,),
        in_specs=[_const_spec(a.shape) for a in args],
        out_specs=tuple(_const_spec((n, w)) for w in widths),
        out_shape=tuple(jax.ShapeDtypeStruct((n, w), F32) for w in widths),
        compiler_params=pltpu.CompilerParams(vmem_limit_bytes=VMEM_LIMIT),
        name="sample_inproj",
    )(*args)


def _sample_state_kernel(scal_ref, aq_ref, ak_ref, av_ref, rq_ref, rk_ref, rv_ref,
                         kbuf_ref, vbuf_ref, st_ref,
                         att_ref, ret_ref, kout_ref, vout_ref, stout_ref):
    nb = SAMPLE_CHUNK
    rows = nb * WINDOW

    def shift_in(buf_ref, new):
        flat = buf_ref[...].reshape(rows, LANES)
        rolled = pltpu.roll(flat, rows - 1, 0)
        rolled = rolled.reshape(nb, WINDOW, LANES)
        pos = lax.broadcasted_iota(jnp.int32, (nb, WINDOW, LANES), 1)
        return jnp.where(pos == WINDOW - 1, new[:, None, :], rolled)

    knew = shift_in(kbuf_ref, ak_ref[...])
    vnew = shift_in(vbuf_ref, av_ref[...])
    kout_ref[...] = knew
    vout_ref[...] = vnew

    aq = aq_ref[...]
    qall = _expand_q([aq[:, c * LANES:(c + 1) * LANES] for c in range(ATT_Q_W // LANES)], nb)
    s = _dot_nt(qall.astype(BF16), knew.reshape(rows, LANES).astype(BF16))
    r = lax.broadcasted_iota(jnp.int32, (nb, rows), 0)
    c = lax.broadcasted_iota(jnp.int32, (nb, rows), 1)
    own = (c // WINDOW) == r
    probs = [_sink_softmax(s[h * nb:(h + 1) * nb], own, scal_ref[h]) for h in range(ATT_HEADS)]
    o = _dot(jnp.concatenate(probs, axis=0).astype(BF16), vnew.reshape(rows, LANES).astype(BF16))
    att_ref[...] = _collect_heads(o, nb)

    qt = rq_ref[...].T
    kt = rk_ref[...].T
    rv = rv_ref[...]
    for bi in range(nb):
        for h in range(RET_HEADS):
            hs = slice(h * RET_DK, (h + 1) * RET_DK)
            s_new = st_ref[bi, h] * scal_ref[ATT_HEADS + h] + kt[hs, bi:bi + 1] * rv[bi:bi + 1, hs]
            stout_ref[bi, h] = s_new
            ret_ref[bi:bi + 1, hs] = jnp.sum(qt[hs, bi:bi + 1] * s_new, axis=0, keepdims=True)


def _sample_state(scal, aq, ak, av, rq, rk, rv, kbuf, vbuf, st):
    n = aq.shape[0]
    nb = SAMPLE_CHUNK
    row = lambda w: pl.BlockSpec((nb, w), lambda i: (i, 0))
    buf = pl.BlockSpec((nb, WINDOW, LANES), lambda i: (i, 0, 0))
    stt = pl.BlockSpec((nb, RET_HEADS, RET_DK, RET_DV), lambda i: (i, 0, 0, 0))
    return pl.pallas_call(
        _sample_state_kernel,
        grid=(n // nb,),
        in_specs=[pl.BlockSpec(memory_space=pltpu.SMEM), row(ATT_Q_W), row(LANES), row(LANES),
                  row(RET_W), row(RET_W), row(RET_W), buf, buf, stt],
        out_specs=(row(ATT_Q_W), row(RET_W), buf, buf, stt),
        out_shape=(jax.ShapeDtypeStruct((n, ATT_Q_W), F32), jax.ShapeDtypeStruct((n, RET_W), F32),
                   jax.ShapeDtypeStruct(kbuf.shape, F32), jax.ShapeDtypeStruct(vbuf.shape, F32),
                   jax.ShapeDtypeStruct(st.shape, F32)),
        compiler_params=pltpu.CompilerParams(
            dimension_semantics=("arbitrary",), vmem_limit_bytes=VMEM_LIMIT),
        name="sample_state",
    )(scal, aq, ak, av, rq, rk, rv, kbuf, vbuf, st)


def _sample_post_kernel(x_ref, att_ref, ret_ref, rg_ref, cnt_in_ref,
                        g_ret_ref, w_out_ref, b_out_ref, g_mlp_ref, wr_hi_ref, wr_lo_ref, b_router_ref,
                        x1_ref, h2_ref, meta_ref, cnt_out_ref):
    x = x_ref[...]
    ret_out = _group_norm_gate(ret_ref[...], rg_ref[...], g_ret_ref[...])
    mixed = jnp.concatenate([att_ref[...], ret_out], axis=1).astype(BF16)
    x1 = x + _dot(mixed, w_out_ref[...]) + b_out_ref[...]
    x1_ref[...] = x1
    h2, meta, c = _route(x1, g_mlp_ref[...], wr_hi_ref[...], wr_lo_ref[...], b_router_ref[...],
                         cnt_in_ref[0:1, :])
    h2_ref[...] = h2
    meta_ref[...] = meta
    cnt_out_ref[...] = jnp.broadcast_to(c, cnt_out_ref.shape)


def _sample_post(x, att, ret, rg, cnt, wts):
    n = x.shape[0]
    vm = [x, att, ret, rg, cnt, *wts]
    widths = (D_MODEL, D_MODEL, LANES)
    return pl.pallas_call(
        _sample_post_kernel,
        grid=(1,),
        in_specs=[_const_spec(a.shape) for a in vm],
        out_specs=tuple(_const_spec((n, w)) for w in widths) + (_const_spec((8, LANES)),),
        out_shape=tuple(jax.ShapeDtypeStruct((n, w), F32) for w in widths)
                  + (jax.ShapeDtypeStruct((8, LANES), F32),),
        compiler_params=pltpu.CompilerParams(vmem_limit_bytes=VMEM_LIMIT),
        name="sample_post",
    )(*vm)


def _row_copy(src, src_row, dst, dst_row, sem):
    return pltpu.make_async_copy(src.at[pl.ds(src_row, 1)], dst.at[pl.ds(dst_row, 1)], sem)


def _for_each_row(n_tok, fn):
    def body(g, carry):
        for u in range(ROW_UNROLL):
            for k in range(TOP_K):
                fn(g * ROW_UNROLL + u, k)
        return carry

    lax.fori_loop(0, n_tok // ROW_UNROLL, body, 0)


def _dispatch_kernel(prompt_steps, n_dec, pos_p, pos_s, h2p_ref, h2s_ref, xs_hbm, sem_rows):
    i = pl.program_id(0)

    def scatter_rows(src_ref, n_tok, pos_ref, first_idx):
        def copy(t, k):
            return _row_copy(src_ref, t, xs_hbm, pos_ref[first_idx + t * TOP_K + k], sem_rows)

        _for_each_row(n_tok, lambda t, k: copy(t, k).start(priority=k % 2))
        _for_each_row(n_tok, lambda t, k: _row_copy(src_ref, 0, xs_hbm, 0, sem_rows).wait())

    @pl.when(i < prompt_steps)
    def _():
        scatter_rows(h2p_ref, DISPATCH_TOKENS, pos_p, i * (DISPATCH_TOKENS * TOP_K))

    @pl.when(i == prompt_steps)
    def _():
        scatter_rows(h2s_ref, n_dec, pos_s, 0)


def _dispatch(pos_p, pos_s, h2_p, h2_s):
    n_prompt, n_dec = h2_p.shape[0], h2_s.shape[0]
    prompt_steps = n_prompt // DISPATCH_TOKENS
    grid_spec = pltpu.PrefetchScalarGridSpec(
        num_scalar_prefetch=2,
        grid=(prompt_steps + 1,),
        in_specs=[pl.BlockSpec((DISPATCH_TOKENS, D_MODEL),
                               lambda i, *_: (jnp.minimum(i, prompt_steps - 1), 0)),
                  pl.BlockSpec(h2_s.shape, lambda i, *_: (0, 0))],
        out_specs=pl.BlockSpec(memory_space=pl.ANY),
        scratch_shapes=[pltpu.SemaphoreType.DMA],
    )
    return pl.pallas_call(
        functools.partial(_dispatch_kernel, prompt_steps, n_dec),
        grid_spec=grid_spec,
        out_shape=jax.ShapeDtypeStruct(((n_prompt + n_dec) * TOP_K, D_MODEL), F32),
        compiler_params=pltpu.CompilerParams(
            dimension_semantics=("arbitrary",), vmem_limit_bytes=VMEM_LIMIT),
        name="dispatch",
    )(pos_p, pos_s, h2_p, h2_s)


def _experts_kernel(v_tile, v_expert, v_lo, v_hi, v_newtile, v_newexp, n_visits,
                    xs_ref, w_in_ref, b_in_ref, w_out_ref, b_out_ref, y_ref, w_in_bf, w_out_bf):
    v = pl.program_id(0)

    @pl.when(v < n_visits[0])
    def _():
        @pl.when(v_newexp[v] == 1)
        def _():
            w_in_bf[...] = w_in_ref[0].astype(BF16)
            w_out_bf[...] = w_out_ref[0].astype(BF16)

        x = xs_ref[...].astype(BF16)
        hg = _dot(x, w_in_bf[...]) + b_in_ref[0]
        gate = jnp.minimum(hg[:, :D_FF], SWIGLU_LIMIT)
        lin = jnp.clip(hg[:, D_FF:], -SWIGLU_LIMIT, SWIGLU_LIMIT)
        act = gate * jax.nn.sigmoid(SWIGLU_ALPHA * gate) * (lin + 1.0)
        y = _dot(act.astype(BF16), w_out_bf[...]) + b_out_ref[0]
        row = lax.broadcasted_iota(jnp.int32, y.shape, 0)
        mine = (row >= v_lo[v]) & (row < v_hi[v])

        @pl.when(v_newtile[v] == 1)
        def _():
            y_ref[...] = jnp.where(mine, y, 0.0)

        @pl.when(v_newtile[v] == 0)
        def _():
            y_ref[...] = jnp.where(mine, y, y_ref[...])


def _experts(visits, xs, w_e_in, b_e_in, w_e_out, b_e_out):
    n_rows = xs.shape[0]
    max_visits = n_rows // EXPERT_ROWS + N_EXPERTS - 1
    grid_spec = pltpu.PrefetchScalarGridSpec(
        num_scalar_prefetch=7,
        grid=(max_visits,),
        in_specs=[
            pl.BlockSpec((EXPERT_ROWS, D_MODEL), lambda v, vt, ve, *_: (vt[v], 0)),
            pl.BlockSpec((1, D_MODEL, 2 * D_FF), lambda v, vt, ve, *_: (ve[v], 0, 0)),
            pl.BlockSpec((1, 1, 2 * D_FF), lambda v, vt, ve, *_: (ve[v], 0, 0)),
            pl.BlockSpec((1, D_FF, D_MODEL), lambda v, vt, ve, *_: (ve[v], 0, 0)),
            pl.BlockSpec((1, 1, D_MODEL), lambda v, vt, ve, *_: (ve[v], 0, 0)),
        ],
        out_specs=pl.BlockSpec((EXPERT_ROWS, D_MODEL), lambda v, vt, ve, *_: (vt[v], 0)),
        scratch_shapes=[pltpu.VMEM((D_MODEL, 2 * D_FF), BF16), pltpu.VMEM((D_FF, D_MODEL), BF16)],
    )
    return pl.pallas_call(
        _experts_kernel,
        grid_spec=grid_spec,
        out_shape=jax.ShapeDtypeStruct((n_rows, D_MODEL), F32),
        compiler_params=pltpu.CompilerParams(
            dimension_semantics=("arbitrary",), vmem_limit_bytes=VMEM_LIMIT),
        name="experts",
    )(*visits, xs, w_e_in, b_e_in.reshape(N_EXPERTS, 1, 2 * D_FF), w_e_out,
      b_e_out.reshape(N_EXPERTS, 1, D_MODEL))


def _visit_schedule(counts, n_rows):
    n_tiles = n_rows // EXPERT_ROWS
    max_visits = n_tiles + N_EXPERTS - 1
    ends = jnp.cumsum(counts)
    starts = ends - counts
    first_tile = starts // EXPERT_ROWS
    tiles_e = jnp.where(counts > 0, (ends - 1) // EXPERT_ROWS - first_tile + 1, 0)
    vis_end = jnp.cumsum(tiles_e)
    vis_start = vis_end - tiles_e
    total = vis_end[-1]
    v = jnp.minimum(jnp.arange(max_visits, dtype=jnp.int32), total - 1)
    e = jnp.sum((vis_end[None, :] <= v[:, None]).astype(jnp.int32), axis=1)
    is_e = e[:, None] == jnp.arange(N_EXPERTS, dtype=jnp.int32)[None, :]
    pick = lambda table: jnp.sum(jnp.where(is_e, table[None, :], 0), axis=1)
    tile = pick(first_tile) + (v - pick(vis_start))
    lo = jnp.maximum(pick(starts), tile * EXPERT_ROWS) - tile * EXPERT_ROWS
    hi = jnp.minimum(pick(ends), (tile + 1) * EXPERT_ROWS) - tile * EXPERT_ROWS
    prev_tile = jnp.concatenate([jnp.full((1,), -1, jnp.int32), tile[:-1]])
    prev_e = jnp.concatenate([jnp.full((1,), -1, jnp.int32), e[:-1]])
    as_i32 = lambda a: a.astype(jnp.int32)
    return (as_i32(tile), as_i32(e), as_i32(lo), as_i32(hi), as_i32(tile != prev_tile),
            as_i32(e != prev_e), as_i32(total).reshape(1))


def _combine_kernel(tokens, pos_ref, x1_ref, meta_ref, p_ref, ys_hbm,
                    g_ple_ref, w_pg_ref, w_ple_ref, g_final_ref, y_ref, rows, sem_rows):
    i, n = pl.program_id(0), pl.num_programs(0)

    def gather_block(block, slot):
        def copy(t, k):
            src_row = pos_ref[(block * tokens + t) * TOP_K + k]
            return _row_copy(ys_hbm, src_row, rows.at[slot, k], t, sem_rows.at[slot])

        _for_each_row(tokens, lambda t, k: copy(t, k).start(priority=k % 2))

    @pl.when(i == 0)
    def _():
        gather_block(0, 0)

    @pl.when(i + 1 < n)
    def _():
        gather_block(i + 1, (i + 1) % 2)

    slot = i % 2
    _for_each_row(tokens, lambda t, k: _row_copy(ys_hbm, 0, rows.at[slot, 0], 0, sem_rows.at[slot]).wait())

    meta = meta_ref[...]
    x2 = x1_ref[...]
    for k in range(TOP_K):
        x2 = x2 + meta[:, META_GATE + k:META_GATE + k + 1] * rows[slot, k]
    hp = _rms(x2, g_ple_ref[...]).astype(BF16)
    gate = jax.nn.sigmoid(_dot(hp, w_pg_ref[...]))
    x3 = x2 + _dot(p_ref[...].astype(BF16), w_ple_ref[...]) * gate
    y_ref[...] = _rms(x3, g_final_ref[...])


def _combine(tokens, pos, x1, meta, p, ys, wts):
    n_out = x1.shape[0]
    tok = lambda w: pl.BlockSpec((tokens, w), lambda i, *_: (i, 0))
    grid_spec = pltpu.PrefetchScalarGridSpec(
        num_scalar_prefetch=1,
        grid=(n_out // tokens,),
        in_specs=[tok(D_MODEL), tok(LANES), tok(PLE_DIM), pl.BlockSpec(memory_space=pl.ANY)]
                 + [pl.BlockSpec(w.shape, lambda i, *_: (0, 0)) for w in wts],
        out_specs=tok(D_MODEL),
        scratch_shapes=[pltpu.VMEM((2, TOP_K, tokens, D_MODEL), F32), pltpu.SemaphoreType.DMA((2,))],
    )
    return pl.pallas_call(
        functools.partial(_combine_kernel, tokens),
        grid_spec=grid_spec,
        out_shape=jax.ShapeDtypeStruct((n_out, D_MODEL), F32),
        compiler_params=pltpu.CompilerParams(
            dimension_semantics=("arbitrary",), vmem_limit_bytes=VMEM_LIMIT),
        name="combine_ple",
    )(pos, x1, meta, p, ys, *wts)


def _rope_tables(pos, n_rows):
    out = []
    for dim in (ATT_HEAD_DIM, RET_DK):
        half = dim // 2
        inv = ROPE_THETA ** (-np.arange(half, dtype=np.float64) / half)
        ang = np.asarray(pos, np.float64)[:, None] * inv[None, :]
        cos, sin = np.cos(ang), np.sin(ang)
        reps = LANES // dim
        cos_row = np.tile(np.concatenate([cos, cos], axis=-1), (1, reps))
        sin_row = np.tile(np.concatenate([-sin, sin], axis=-1), (1, reps))
        out += [jnp.asarray(np.broadcast_to(cos_row, (n_rows, LANES)), F32),
                jnp.asarray(np.broadcast_to(sin_row, (n_rows, LANES)), F32)]
    return out


def _retention_tables():
    log_g = np.log(1.0 - 2.0 ** (-5.0 - np.arange(RET_HEADS, dtype=np.float64)))
    t = np.arange(BLK, dtype=np.float64)
    rel = t[:, None] - t[None, :]
    intra = np.where(rel >= 0, np.exp(np.maximum(rel, 0.0)[None] * log_g[:, None, None]), 0.0)
    cross = np.exp((t + 1.0)[None, :] * log_g[:, None])
    kdec = np.exp((BLK - 1.0 - t)[None, :] * log_g[:, None])
    lane_bcast = lambda a: np.broadcast_to(a[:, :, None], (RET_HEADS, BLK, LANES))
    tabs = tuple(jnp.asarray(a, F32) for a in (intra, lane_bcast(cross), lane_bcast(kdec)))
    return log_g, tabs


def kernel(x_prompt, x_sample, state_swa_k, state_swa_v, state_ret, p_prompt, p_sample, g_mix, w_in, b_in,
           attn_sinks, g_ret_norm, w_out, b_out, g_mlp, w_router, b_router, w_e_in, b_e_in, w_e_out,
           b_e_out, g_ple, w_ple_gate, w_ple, g_final):
    bsz, seq, _ = x_prompt.shape
    n_dec = x_sample.shape[0]
    n_prompt = bsz * seq
    assert x_sample.shape[1] == 1 and g_mix.shape[0] == 1
    assert state_swa_k.shape[2] == WINDOW and n_dec <= DISPATCH_TOKENS

    row = lambda a: a.reshape(1, -1)
    log_g, ret_tabs = _retention_tables()
    sinks = attn_sinks[0].astype(F32)
    scal_prompt = jnp.concatenate([sinks, jnp.asarray(np.exp(BLK * log_g), F32)])
    scal_sample = jnp.concatenate([sinks, jnp.asarray(np.exp(log_g), F32)])

    w_in_bf = w_in[0].astype(BF16)
    w_out_bf = w_out[0].astype(BF16)
    wr = jnp.pad(w_router[0], ((0, 0), (0, LANES - N_EXPERTS)))
    wr_hi = wr.astype(BF16)
    wr_lo = (wr - wr_hi.astype(F32)).astype(BF16)
    br = jnp.pad(b_router[0], (0, LANES - N_EXPERTS)).reshape(1, LANES)
    post_wts = (row(g_ret_norm[0]), w_out_bf, row(b_out[0]), row(g_mlp[0]), wr_hi, wr_lo, br)

    tabs_p = _rope_tables(np.arange(seq), seq)
    x1_p, h2_p, meta_p, cnt, kp, vp, st_p = _mixer_prompt(
        scal_prompt, x_prompt, tabs_p, ret_tabs, (row(g_mix[0]), w_in_bf, row(b_in[0])) + post_wts)

    tabs_s = _rope_tables(np.full((1,), PAST_LEN), n_dec)
    xs2 = x_sample.reshape(n_dec, D_MODEL)
    aq, ak, av, rq, rk, rv, rg = _sample_inproj(xs2, tabs_s, row(g_mix[0]), w_in_bf, row(b_in[0]))
    att, ret, ks, vs, st_s = _sample_state(
        scal_sample, aq, ak, av, rq, rk, rv,
        state_swa_k[0].reshape(n_dec, WINDOW, LANES), state_swa_v[0].reshape(n_dec, WINDOW, LANES),
        state_ret[0])
    x1_s, h2_s, meta_s, cnt = _sample_post(xs2, att, ret, rg, cnt, post_wts)

    counts = cnt[0, :N_EXPERTS].astype(jnp.int32)
    starts = jnp.cumsum(counts) - counts
    def sorted_rows(meta):
        sel = meta[:, META_IDX:META_IDX + TOP_K].astype(jnp.int32)
        rank = meta[:, META_RANK:META_RANK + TOP_K].astype(jnp.int32)
        is_e = sel[:, :, None] == jnp.arange(N_EXPERTS, dtype=jnp.int32)[None, None, :]
        return (jnp.sum(jnp.where(is_e, starts[None, None, :], 0), axis=-1) + rank).reshape(-1)

    pos_p, pos_s = sorted_rows(meta_p), sorted_rows(meta_s)

    xs_sorted = _dispatch(pos_p, pos_s, h2_p, h2_s)
    visits = _visit_schedule(counts, (n_prompt + n_dec) * TOP_K)
    ys_sorted = _experts(visits, xs_sorted, w_e_in[0], b_e_in[0], w_e_out[0], b_e_out[0])

    ple_wts = (row(g_ple[0]), w_ple_gate[0].astype(BF16), w_ple[0].astype(BF16), row(g_final))
    y_p = _combine(COMBINE_TOKENS, pos_p, x1_p, meta_p, p_prompt[0].reshape(n_prompt, PLE_DIM),
                   ys_sorted, ple_wts)
    y_s = _combine(n_dec, pos_s, x1_s, meta_s, p_sample[0].reshape(n_dec, PLE_DIM), ys_sorted, ple_wts)

    kv_shape = (1, bsz, WINDOW, ATT_KV_HEADS, ATT_HEAD_DIM)
    dec_shape = (1, n_dec, WINDOW, ATT_KV_HEADS, ATT_HEAD_DIM)
    return (y_p.reshape(bsz, seq, D_MODEL), y_s.reshape(n_dec, 1, D_MODEL),
            kp.reshape(kv_shape), vp.reshape(kv_shape), st_p[None],
            ks.reshape(dec_shape), vs.reshape(dec_shape), st_s[None])
```

```python
import functools

import jax
import jax.numpy as jnp
import numpy as np
from jax import lax
from jax.experimental import pallas as pl
from jax.experimental.pallas import tpu as pltpu

D_MODEL = 1024
PAST_LEN = 16384
PLE_DIM = 256
ATT_HEADS = 8
ATT_KV_HEADS = 2
ATT_HEAD_DIM = 64
ATT_GROUP = ATT_HEADS // ATT_KV_HEADS
WINDOW = 128
BLK = 128
RET_HEADS = 4
RET_DK = 128
RET_DV = 128
ROPE_THETA = 10000.0
N_EXPERTS = 32
TOP_K = 4
D_FF = 1024
SWIGLU_LIMIT = 7.0
SWIGLU_ALPHA = 1.702
NORM_EPS = 1e-5
GN_EPS = 1e-6
NEG_INF = -1e30

ATT_Q_W = ATT_HEADS * ATT_HEAD_DIM
ATT_KV_W = ATT_KV_HEADS * ATT_HEAD_DIM
RET_W = RET_HEADS * RET_DK
IN_W = ATT_Q_W + 2 * ATT_KV_W + 4 * RET_W
O_AQ = 0
O_AK = ATT_Q_W
O_AV = O_AK + ATT_KV_W
O_RQ = O_AV + ATT_KV_W
O_RK = O_RQ + RET_W
O_RV = O_RK + RET_W
O_RG = O_RV + RET_W

LANES = 128
META_IDX, META_RANK, META_GATE = 0, 4, 8

MIX_TOKENS = 256
DISPATCH_TOKENS = 256
EXPERT_ROWS = 256
COMBINE_TOKENS = 256
SAMPLE_CHUNK = 16
ROW_UNROLL = 4
FF_CHUNK = 256
VMEM_LIMIT = 56 * 1024 * 1024

F32 = jnp.float32
BF16 = jnp.bfloat16


def _dot(a, b):
    return jnp.dot(a, b, preferred_element_type=F32)


def _dot_nt(a, b):
    return lax.dot_general(a, b, (((1,), (1,)), ((), ())), preferred_element_type=F32)


def _dot_tn(a, b):
    return lax.dot_general(a, b, (((0,), (0,)), ((), ())), preferred_element_type=F32)


TOKEN_TILE = D_MODEL // LANES


def _store_token_tiles(ref, first_token, rows):
    n = rows.shape[0]
    for c in range(TOKEN_TILE):
        ref[pl.ds(first_token * TOKEN_TILE + c, n, stride=TOKEN_TILE), :] = rows[:, c * LANES:(c + 1) * LANES]


def _load_token_tiles(ref, n):
    return jnp.concatenate([ref[pl.ds(c, n, stride=TOKEN_TILE), :] for c in range(TOKEN_TILE)], axis=1)


def _rms(x, g):
    return x * lax.rsqrt(jnp.mean(x * x, axis=-1, keepdims=True) + NORM_EPS) * g


def _rope(col, cos, sin_signed, half):
    lane = lax.broadcasted_iota(jnp.int32, col.shape, 1)
    fwd = pltpu.roll(col, LANES - half, 1)
    bwd = pltpu.roll(col, half, 1)
    swapped = jnp.where((lane % (2 * half)) < half, fwd, bwd)
    return col * cos + swapped * sin_signed


def _in_proj(x, g_mix, w_in, b_in, cos64, sin64, cos128, sin128):
    h = _rms(x, g_mix).astype(BF16)
    z = _dot(h, w_in) + b_in

    def cols(off, width):
        return [z[:, off + c * LANES: off + (c + 1) * LANES] for c in range(width // LANES)]

    scale_q = ATT_HEAD_DIM ** -0.5
    aq = [_rope(c, cos64, sin64, ATT_HEAD_DIM // 2) * scale_q for c in cols(O_AQ, ATT_Q_W)]
    ak = _rope(z[:, O_AK:O_AK + LANES], cos64, sin64, ATT_HEAD_DIM // 2)
    av = z[:, O_AV:O_AV + LANES]
    rq = [_rope(c, cos128, sin128, RET_DK // 2) for c in cols(O_RQ, RET_W)]
    rk = [_rope(c, cos128, sin128, RET_DK // 2) * (RET_DK ** -0.5) for c in cols(O_RK, RET_W)]
    rv = cols(O_RV, RET_W)
    rg = z[:, O_RG:O_RG + RET_W]
    return aq, ak, av, rq, rk, rv, rg


def _expand_q(aq_cols, rows):
    lane = lax.broadcasted_iota(jnp.int32, (rows, LANES), 1)
    low = lane < ATT_HEAD_DIM
    pieces = []
    for h in range(ATT_HEADS):
        col, half, kv = aq_cols[h // 2], h % 2, h // ATT_GROUP
        if half != kv:
            col = pltpu.roll(col, ATT_HEAD_DIM, 1)
        pieces.append(jnp.where(low if kv == 0 else jnp.logical_not(low), col, 0.0))
    return jnp.concatenate(pieces, axis=0)


def _collect_heads(o, rows):
    lane = lax.broadcasted_iota(jnp.int32, (rows, LANES), 1)
    low = lane < ATT_HEAD_DIM
    out = []
    for c in range(ATT_HEADS // 2):
        pieces = []
        for half in (0, 1):
            h = 2 * c + half
            oh = o[h * rows:(h + 1) * rows]
            if h // ATT_GROUP != half:
                oh = pltpu.roll(oh, ATT_HEAD_DIM, 1)
            pieces.append(oh)
        out.append(jnp.where(low, pieces[0], pieces[1]))
    return jnp.concatenate(out, axis=1)


def _sink_softmax(s, mask, sink):
    s = jnp.where(mask, s, NEG_INF)
    m = jnp.maximum(jnp.max(s, axis=-1, keepdims=True), sink)
    e = jnp.exp(s - m)
    den = jnp.sum(e, axis=-1, keepdims=True) + jnp.exp(sink - m)
    return e * (1.0 / den)


def _group_norm_gate(ret, rg, g_ret):
    cols = []
    for h in range(RET_HEADS):
        of = ret[:, h * RET_DV:(h + 1) * RET_DV]
        d = of - jnp.mean(of, axis=-1, keepdims=True)
        var = jnp.mean(d * d, axis=-1, keepdims=True)
        cols.append(d * lax.rsqrt(var + GN_EPS))
    y = jnp.concatenate(cols, axis=1) * g_ret
    return y * (rg * jax.nn.sigmoid(rg))


def _route(x1, g_mlp, wr_hi, wr_lo, b_router, cnt):
    n = x1.shape[0]
    h2 = _rms(x1, g_mlp)
    hi = h2.astype(BF16)
    lo = (h2 - hi.astype(F32)).astype(BF16)
    logits = _dot(hi, wr_hi) + (_dot(hi, wr_lo) + _dot(lo, wr_hi)) + b_router
    lane = lax.broadcasted_iota(jnp.int32, (n, LANES), 1)
    work = jnp.where(lane < N_EXPERTS, logits, -jnp.inf)
    idxs, vals = [], []
    for _ in range(TOP_K):
        m = jnp.max(work, axis=-1, keepdims=True)
        idx = jnp.min(jnp.where(work == m, lane, LANES), axis=-1, keepdims=True)
        idxs.append(idx)
        vals.append(m)
        work = jnp.where(lane == idx, -jnp.inf, work)
    es = [jnp.exp(v - vals[0]) for v in vals]
    den = es[0] + es[1] + es[2] + es[3]
    gates = [e / den for e in es]

    sel = (lane == idxs[0]) | (lane == idxs[1]) | (lane == idxs[2]) | (lane == idxs[3])
    onehot = jnp.where(sel, 1.0, 0.0)
    r = lax.broadcasted_iota(jnp.int32, (n, n), 0)
    c = lax.broadcasted_iota(jnp.int32, (n, n), 1)
    below = jnp.where(r > c, 1.0, 0.0).astype(BF16)
    before = _dot(below, onehot.astype(BF16)) + cnt
    meta = jnp.zeros((n, LANES), F32)
    for k in range(TOP_K):
        rank = jnp.sum(jnp.where(lane == idxs[k], before, 0.0), axis=-1, keepdims=True)
        meta = jnp.where(lane == META_IDX + k, idxs[k].astype(F32), meta)
        meta = jnp.where(lane == META_RANK + k, rank, meta)
        meta = jnp.where(lane == META_GATE + k, gates[k], meta)
    return h2, meta, cnt + jnp.sum(onehot, axis=0, keepdims=True)


def _mixer_prompt_kernel(scal_ref, x_ref, cos64_ref, sin64_ref, cos128_ref, sin128_ref,
                         intra_ref, cross_ref, kdec_ref,
                         g_mix_ref, w_in_ref, b_in_ref, g_ret_ref, w_out_ref, b_out_ref,
                         g_mlp_ref, wr_hi_ref, wr_lo_ref, b_router_ref,
                         x1_ref, h2_ref, meta_ref, cnt_out_ref, kp_ref, vp_ref, stp_ref,
                         prevk, prevv, st, cnt):
    b, n = pl.program_id(0), pl.program_id(1)

    @pl.when((b == 0) & (n == 0))
    def _():
        cnt[...] = jnp.zeros_like(cnt)

    @pl.when(n == 0)
    def _():
        prevk[...] = jnp.zeros_like(prevk)
        prevv[...] = jnp.zeros_like(prevv)
        st[...] = jnp.zeros_like(st)

    x = x_ref[...]
    aq, ak, av, rq, rk, rv, rg = _in_proj(
        x, g_mix_ref[...], w_in_ref[...], b_in_ref[...],
        cos64_ref[...], sin64_ref[...], cos128_ref[...], sin128_ref[...])

    row = lax.broadcasted_iota(jnp.int32, (BLK, 2 * BLK), 0)
    col = lax.broadcasted_iota(jnp.int32, (BLK, 2 * BLK), 1)
    mask_prev = (col < BLK) & (col > row)
    mask_cur = (col >= BLK) & (col - BLK <= row)

    k_prev, v_prev = prevk[...], prevv[...]
    mixed = []
    for j in range(MIX_TOKENS // BLK):
        sl = slice(j * BLK, (j + 1) * BLK)
        kk = jnp.concatenate([k_prev, ak[sl]], axis=0).astype(BF16)
        vv = jnp.concatenate([v_prev, av[sl]], axis=0).astype(BF16)
        qall = _expand_q([cq[sl] for cq in aq], BLK).astype(BF16)
        s = _dot_nt(qall, kk)
        mask = mask_cur | mask_prev if j > 0 else mask_cur | (mask_prev & (n > 0))
        probs = [_sink_softmax(s[h * BLK:(h + 1) * BLK], mask, scal_ref[h]) for h in range(ATT_HEADS)]
        o = _dot(jnp.concatenate(probs, axis=0).astype(BF16), vv)
        att = _collect_heads(o, BLK)
        k_prev, v_prev = ak[sl], av[sl]

        ret = []
        for h in range(RET_HEADS):
            qb, kh, vb = rq[h][sl].astype(BF16), rk[h][sl], rv[h][sl].astype(BF16)
            sc = _dot_nt(qb, kh.astype(BF16)) * intra_ref[h]
            sth = st[h]
            o_h = _dot(sc.astype(BF16), vb) + _dot(qb, sth.astype(BF16)) * cross_ref[h]
            st[h] = sth * scal_ref[ATT_HEADS + h] + _dot_tn((kh * kdec_ref[h]).astype(BF16), vb)
            ret.append(o_h)
        ret_out = _group_norm_gate(jnp.concatenate(ret, axis=1), rg[sl], g_ret_ref[...])
        mixed.append(jnp.concatenate([att, ret_out], axis=1))

    mixed = jnp.concatenate(mixed, axis=0).astype(BF16)
    x1 = x + _dot(mixed, w_out_ref[...]) + b_out_ref[...]
    x1_ref[...] = x1

    c = cnt[0:1, :]
    for j in range(MIX_TOKENS // BLK):
        sl = slice(j * BLK, (j + 1) * BLK)
        h2, meta, c = _route(x1[sl], g_mlp_ref[...], wr_hi_ref[...], wr_lo_ref[...], b_router_ref[...], c)
        _store_token_tiles(h2_ref, j * BLK, h2)
        meta_ref[sl, :] = meta
    cnt[...] = jnp.broadcast_to(c, cnt.shape)
    cnt_out_ref[...] = cnt[...]

    prevk[...] = k_prev
    prevv[...] = v_prev
    kp_ref[0] = k_prev
    vp_ref[0] = v_prev
    stp_ref[0] = st[...]


def _const_spec(shape):
    nd = len(shape)
    return pl.BlockSpec(shape, lambda *_: (0,) * nd)


def _mixer_prompt(scal, x, tabs, ret_tabs, wts):
    bsz, seq, _ = x.shape
    steps = seq // MIX_TOKENS
    x2 = x.reshape(bsz * seq, D_MODEL)
    tok = lambda w: pl.BlockSpec((MIX_TOKENS, w), lambda b, n: (b * steps + n, 0))
    pos = lambda w: pl.BlockSpec((MIX_TOKENS, w), lambda b, n: (n, 0))
    in_specs = ([pl.BlockSpec(memory_space=pltpu.SMEM), tok(D_MODEL)]
                + [pos(LANES)] * 4
                + [_const_spec(t.shape) for t in ret_tabs]
                + [_const_spec(w.shape) for w in wts])
    out_shape = (
        jax.ShapeDtypeStruct((bsz * seq, D_MODEL), F32),
        jax.ShapeDtypeStruct((bsz * seq * TOKEN_TILE, LANES), F32),
        jax.ShapeDtypeStruct((bsz * seq, LANES), F32),
        jax.ShapeDtypeStruct((8, LANES), F32),
        jax.ShapeDtypeStruct((bsz, BLK, LANES), F32),
        jax.ShapeDtypeStruct((bsz, BLK, LANES), F32),
        jax.ShapeDtypeStruct((bsz, RET_HEADS, RET_DK, RET_DV), F32),
    )
    out_specs = (
        tok(D_MODEL), pl.BlockSpec((MIX_TOKENS * TOKEN_TILE, LANES), lambda b, n: (b * steps + n, 0)),
        tok(LANES), _const_spec((8, LANES)),
        pl.BlockSpec((1, BLK, LANES), lambda b, n: (b, 0, 0)),
        pl.BlockSpec((1, BLK, LANES), lambda b, n: (b, 0, 0)),
        pl.BlockSpec((1, RET_HEADS, RET_DK, RET_DV), lambda b, n: (b, 0, 0, 0)),
    )
    return pl.pallas_call(
        _mixer_prompt_kernel,
        grid=(bsz, steps),
        in_specs=in_specs,
        out_specs=out_specs,
        out_shape=out_shape,
        scratch_shapes=[pltpu.VMEM((BLK, LANES), F32), pltpu.VMEM((BLK, LANES), F32),
                        pltpu.VMEM((RET_HEADS, RET_DK, RET_DV), F32), pltpu.VMEM((8, LANES), F32)],
        compiler_params=pltpu.CompilerParams(
            dimension_semantics=("arbitrary", "arbitrary"), vmem_limit_bytes=VMEM_LIMIT),
        name="mixer_prompt",
    )(scal, x2, *tabs, *ret_tabs, *wts)


def _sample_inproj_kernel(x_ref, cos64_ref, sin64_ref, cos128_ref, sin128_ref,
                          g_mix_ref, w_in_ref, b_in_ref,
                          aq_ref, ak_ref, av_ref, rq_ref, rk_ref, rv_ref, rg_ref):
    aq, ak, av, rq, rk, rv, rg = _in_proj(
        x_ref[...], g_mix_ref[...], w_in_ref[...], b_in_ref[...],
        cos64_ref[...], sin64_ref[...], cos128_ref[...], sin128_ref[...])
    aq_ref[...] = jnp.concatenate(aq, axis=1)
    ak_ref[...] = ak
    av_ref[...] = av
    rq_ref[...] = jnp.concatenate(rq, axis=1)
    rk_ref[...] = jnp.concatenate(rk, axis=1)
    rv_ref[...] = jnp.concatenate(rv, axis=1)
    rg_ref[...] = rg


def _sample_inproj(x, tabs, g_mix, w_in, b_in):
    n = x.shape[0]
    widths = (ATT_Q_W, LANES, LANES, RET_W, RET_W, RET_W, RET_W)
    args = (x, *tabs, g_mix, w_in, b_in)
    return pl.pallas_call(
        _sample_inproj_kernel,
        grid=(1,),
        in_specs=[_const_spec(a.shape) for a in args],
        out_specs=tuple(_const_spec((n, w)) for w in widths),
        out_shape=tuple(jax.ShapeDtypeStruct((n, w), F32) for w in widths),
        compiler_params=pltpu.CompilerParams(vmem_limit_bytes=VMEM_LIMIT),
        name="sample_inproj",
    )(*args)


def _sample_state_kernel(scal_ref, aq_ref, ak_ref, av_ref, rq_ref, rk_ref, rv_ref,
                         kbuf_ref, vbuf_ref, st_ref,
                         att_ref, ret_ref, kout_ref, vout_ref, stout_ref):
    nb = SAMPLE_CHUNK
    rows = nb * WINDOW

    def shift_in(buf_ref, new):
        flat = buf_ref[...].reshape(rows, LANES)
        rolled = pltpu.roll(flat, rows - 1, 0)
        rolled = rolled.reshape(nb, WINDOW, LANES)
        pos = lax.broadcasted_iota(jnp.int32, (nb, WINDOW, LANES), 1)
        return jnp.where(pos == WINDOW - 1, new[:, None, :], rolled)

    knew = shift_in(kbuf_ref, ak_ref[...])
    vnew = shift_in(vbuf_ref, av_ref[...])
    kout_ref[...] = knew
    vout_ref[...] = vnew

    aq = aq_ref[...]
    qall = _expand_q([aq[:, c * LANES:(c + 1) * LANES] for c in range(ATT_Q_W // LANES)], nb)
    s = _dot_nt(qall.astype(BF16), knew.reshape(rows, LANES).astype(BF16))
    r = lax.broadcasted_iota(jnp.int32, (nb, rows), 0)
    c = lax.broadcasted_iota(jnp.int32, (nb, rows), 1)
    own = (c // WINDOW) == r
    probs = [_sink_softmax(s[h * nb:(h + 1) * nb], own, scal_ref[h]) for h in range(ATT_HEADS)]
    o = _dot(jnp.concatenate(probs, axis=0).astype(BF16), vnew.reshape(rows, LANES).astype(BF16))
    att_ref[...] = _collect_heads(o, nb)

    qt = rq_ref[...].T
    kt = rk_ref[...].T
    rv = rv_ref[...]
    for bi in range(nb):
        for h in range(RET_HEADS):
            hs = slice(h * RET_DK, (h + 1) * RET_DK)
            s_new = st_ref[bi, h] * scal_ref[ATT_HEADS + h] + kt[hs, bi:bi + 1] * rv[bi:bi + 1, hs]
            stout_ref[bi, h] = s_new
            ret_ref[bi:bi + 1, hs] = jnp.sum(qt[hs, bi:bi + 1] * s_new, axis=0, keepdims=True)


def _sample_state(scal, aq, ak, av, rq, rk, rv, kbuf, vbuf, st):
    n = aq.shape[0]
    nb = SAMPLE_CHUNK
    row = lambda w: pl.BlockSpec((nb, w), lambda i: (i, 0))
    buf = pl.BlockSpec((nb, WINDOW, LANES), lambda i: (i, 0, 0))
    stt = pl.BlockSpec((nb, RET_HEADS, RET_DK, RET_DV), lambda i: (i, 0, 0, 0))
    return pl.pallas_call(
        _sample_state_kernel,
        grid=(n // nb,),
        in_specs=[pl.BlockSpec(memory_space=pltpu.SMEM), row(ATT_Q_W), row(LANES), row(LANES),
                  row(RET_W), row(RET_W), row(RET_W), buf, buf, stt],
        out_specs=(row(ATT_Q_W), row(RET_W), buf, buf, stt),
        out_shape=(jax.ShapeDtypeStruct((n, ATT_Q_W), F32), jax.ShapeDtypeStruct((n, RET_W), F32),
                   jax.ShapeDtypeStruct(kbuf.shape, F32), jax.ShapeDtypeStruct(vbuf.shape, F32),
                   jax.ShapeDtypeStruct(st.shape, F32)),
        compiler_params=pltpu.CompilerParams(
            dimension_semantics=("arbitrary",), vmem_limit_bytes=VMEM_LIMIT),
        name="sample_state",
    )(scal, aq, ak, av, rq, rk, rv, kbuf, vbuf, st)


def _sample_post_kernel(x_ref, att_ref, ret_ref, rg_ref, cnt_in_ref,
                        g_ret_ref, w_out_ref, b_out_ref, g_mlp_ref, wr_hi_ref, wr_lo_ref, b_router_ref,
                        x1_ref, h2_ref, meta_ref, cnt_out_ref):
    x = x_ref[...]
    ret_out = _group_norm_gate(ret_ref[...], rg_ref[...], g_ret_ref[...])
    mixed = jnp.concatenate([att_ref[...], ret_out], axis=1).astype(BF16)
    x1 = x + _dot(mixed, w_out_ref[...]) + b_out_ref[...]
    x1_ref[...] = x1
    h2, meta, c = _route(x1, g_mlp_ref[...], wr_hi_ref[...], wr_lo_ref[...], b_router_ref[...],
                         cnt_in_ref[0:1, :])
    _store_token_tiles(h2_ref, 0, h2)
    meta_ref[...] = meta
    cnt_out_ref[...] = jnp.broadcast_to(c, cnt_out_ref.shape)


def _sample_post(x, att, ret, rg, cnt, wts):
    n = x.shape[0]
    vm = [x, att, ret, rg, cnt, *wts]
    shapes = ((n, D_MODEL), (n * TOKEN_TILE, LANES), (n, LANES), (8, LANES))
    return pl.pallas_call(
        _sample_post_kernel,
        grid=(1,),
        in_specs=[_const_spec(a.shape) for a in vm],
        out_specs=tuple(_const_spec(s) for s in shapes),
        out_shape=tuple(jax.ShapeDtypeStruct(s, F32) for s in shapes),
        compiler_params=pltpu.CompilerParams(vmem_limit_bytes=VMEM_LIMIT),
        name="sample_post",
    )(*vm)


def _row_copy(src, src_token, dst, dst_token, sem):
    tile = lambda tok: pl.ds(pl.multiple_of(tok * TOKEN_TILE, TOKEN_TILE), TOKEN_TILE)
    return pltpu.make_async_copy(src.at[tile(src_token)], dst.at[tile(dst_token)], sem)


def _for_each_row(n_tok, fn):
    def body(g, carry):
        for u in range(ROW_UNROLL):
            for k in range(TOP_K):
                fn(g * ROW_UNROLL + u, k)
        return carry

    lax.fori_loop(0, n_tok // ROW_UNROLL, body, 0)


def _dispatch_kernel(prompt_steps, n_dec, pos_p, pos_s, h2p_ref, h2s_ref, xs_hbm, sem_rows):
    i = pl.program_id(0)

    def scatter_rows(src_ref, n_tok, pos_ref, first_idx):
        def copy(t, k):
            return _row_copy(src_ref, t, xs_hbm, pos_ref[first_idx + t * TOP_K + k], sem_rows)

        _for_each_row(n_tok, lambda t, k: copy(t, k).start(priority=k % 2))
        _for_each_row(n_tok, lambda t, k: _row_copy(src_ref, 0, xs_hbm, 0, sem_rows).wait())

    @pl.when(i < prompt_steps)
    def _():
        scatter_rows(h2p_ref, DISPATCH_TOKENS, pos_p, i * (DISPATCH_TOKENS * TOP_K))

    @pl.when(i == prompt_steps)
    def _():
        scatter_rows(h2s_ref, n_dec, pos_s, 0)


def _dispatch(pos_p, pos_s, h2_p, h2_s):
    n_prompt, n_dec = h2_p.shape[0] // TOKEN_TILE, h2_s.shape[0] // TOKEN_TILE
    prompt_steps = n_prompt // DISPATCH_TOKENS
    grid_spec = pltpu.PrefetchScalarGridSpec(
        num_scalar_prefetch=2,
        grid=(prompt_steps + 1,),
        in_specs=[pl.BlockSpec((DISPATCH_TOKENS * TOKEN_TILE, LANES),
                               lambda i, *_: (jnp.minimum(i, prompt_steps - 1), 0)),
                  pl.BlockSpec(h2_s.shape, lambda i, *_: (0, 0))],
        out_specs=pl.BlockSpec(memory_space=pl.ANY),
        scratch_shapes=[pltpu.SemaphoreType.DMA],
    )
    return pl.pallas_call(
        functools.partial(_dispatch_kernel, prompt_steps, n_dec),
        grid_spec=grid_spec,
        out_shape=jax.ShapeDtypeStruct(((n_prompt + n_dec) * TOP_K * TOKEN_TILE, LANES), F32),
        compiler_params=pltpu.CompilerParams(
            dimension_semantics=("arbitrary",), vmem_limit_bytes=VMEM_LIMIT),
        name="dispatch",
    )(pos_p, pos_s, h2_p, h2_s)


def _experts_kernel(v_tile, v_expert, v_lo, v_hi, v_newtile, v_newexp, v_next, n_visits,
                    xs_ref, w_in_hbm, b_in_ref, w_out_hbm, b_out_ref, y_ref,
                    w_in_f32, w_out_f32, w_in_bf, w_out_bf, sems):
    v = pl.program_id(0)

    def weight_copies(e):
        return (pltpu.make_async_copy(w_in_hbm.at[e], w_in_f32, sems.at[0]),
                pltpu.make_async_copy(w_out_hbm.at[e], w_out_f32, sems.at[1]))

    @pl.when(v < n_visits[0])
    def _():
        @pl.when(v_newexp[v] == 1)
        def _():
            @pl.when(v == 0)
            def _():
                for cp in weight_copies(v_expert[0]):
                    cp.start()

            for cp in weight_copies(v_expert[v]):
                cp.wait()
            w_in_bf[...] = w_in_f32[...].astype(BF16)
            w_out_bf[...] = w_out_f32[...].astype(BF16)

            @pl.when(v_next[v] >= 0)
            def _():
                for cp in weight_copies(v_next[v]):
                    cp.start()

        x = _load_token_tiles(xs_ref, EXPERT_ROWS).astype(BF16)
        y = jnp.zeros((EXPERT_ROWS, D_MODEL), F32) + b_out_ref[0]
        for f in range(D_FF // FF_CHUNK):
            gc = slice(f * FF_CHUNK, (f + 1) * FF_CHUNK)
            lc = slice(D_FF + f * FF_CHUNK, D_FF + (f + 1) * FF_CHUNK)
            gate = jnp.minimum(_dot(x, w_in_bf[:, gc]) + b_in_ref[0, :, gc], SWIGLU_LIMIT)
            lin = jnp.clip(_dot(x, w_in_bf[:, lc]) + b_in_ref[0, :, lc], -SWIGLU_LIMIT, SWIGLU_LIMIT)
            act = gate * jax.nn.sigmoid(SWIGLU_ALPHA * gate) * (lin + 1.0)
            y = y + _dot(act.astype(BF16), w_out_bf[gc, :])

        row = lax.broadcasted_iota(jnp.int32, (EXPERT_ROWS, LANES), 0)
        mine = (row >= v_lo[v]) & (row < v_hi[v])

        def write_rows(first_visit_of_tile):
            for c in range(TOKEN_TILE):
                rows_c = pl.ds(c, EXPERT_ROWS, stride=TOKEN_TILE)
                keep = 0.0 if first_visit_of_tile else y_ref[rows_c, :]
                y_ref[rows_c, :] = jnp.where(mine, y[:, c * LANES:(c + 1) * LANES], keep)

        @pl.when(v_newtile[v] == 1)
        def _():
            write_rows(True)

        @pl.when(v_newtile[v] == 0)
        def _():
            write_rows(False)


def _experts(visits, xs, w_e_in, b_e_in, w_e_out, b_e_out):
    n_rows = xs.shape[0] // TOKEN_TILE
    max_visits = n_rows // EXPERT_ROWS + N_EXPERTS - 1
    tile_rows = pl.BlockSpec((EXPERT_ROWS * TOKEN_TILE, LANES), lambda v, vt, *_: (vt[v], 0))
    any_spec = pl.BlockSpec(memory_space=pl.ANY)
    grid_spec = pltpu.PrefetchScalarGridSpec(
        num_scalar_prefetch=len(visits),
        grid=(max_visits,),
        in_specs=[
            tile_rows, any_spec,
            pl.BlockSpec((1, 1, 2 * D_FF), lambda v, vt, ve, *_: (ve[v], 0, 0)),
            any_spec,
            pl.BlockSpec((1, 1, D_MODEL), lambda v, vt, ve, *_: (ve[v], 0, 0)),
        ],
        out_specs=tile_rows,
        scratch_shapes=[pltpu.VMEM((D_MODEL, 2 * D_FF), F32), pltpu.VMEM((D_FF, D_MODEL), F32),
                        pltpu.VMEM((D_MODEL, 2 * D_FF), BF16), pltpu.VMEM((D_FF, D_MODEL), BF16),
                        pltpu.SemaphoreType.DMA((2,))],
    )
    return pl.pallas_call(
        _experts_kernel,
        grid_spec=grid_spec,
        out_shape=jax.ShapeDtypeStruct(xs.shape, F32),
        compiler_params=pltpu.CompilerParams(
            dimension_semantics=("arbitrary",), vmem_limit_bytes=VMEM_LIMIT),
        name="experts",
    )(*visits, xs, w_e_in, b_e_in.reshape(N_EXPERTS, 1, 2 * D_FF), w_e_out,
      b_e_out.reshape(N_EXPERTS, 1, D_MODEL))


def _visit_schedule(counts, n_rows):
    n_tiles = n_rows // EXPERT_ROWS
    max_visits = n_tiles + N_EXPERTS - 1
    ends = jnp.cumsum(counts)
    starts = ends - counts
    first_tile = starts // EXPERT_ROWS
    tiles_e = jnp.where(counts > 0, (ends - 1) // EXPERT_ROWS - first_tile + 1, 0)
    vis_end = jnp.cumsum(tiles_e)
    vis_start = vis_end - tiles_e
    total = vis_end[-1]
    v = jnp.minimum(jnp.arange(max_visits, dtype=jnp.int32), total - 1)
    e = jnp.sum((vis_end[None, :] <= v[:, None]).astype(jnp.int32), axis=1)
    is_e = e[:, None] == jnp.arange(N_EXPERTS, dtype=jnp.int32)[None, :]
    pick = lambda table: jnp.sum(jnp.where(is_e, table[None, :], 0), axis=1)
    tile = pick(first_tile) + (v - pick(vis_start))
    lo = jnp.maximum(pick(starts), tile * EXPERT_ROWS) - tile * EXPERT_ROWS
    hi = jnp.minimum(pick(ends), (tile + 1) * EXPERT_ROWS) - tile * EXPERT_ROWS
    prev_tile = jnp.concatenate([jnp.full((1,), -1, jnp.int32), tile[:-1]])
    prev_e = jnp.concatenate([jnp.full((1,), -1, jnp.int32), e[:-1]])
    experts = jnp.arange(N_EXPERTS, dtype=jnp.int32)
    later = (experts[None, :] > experts[:, None]) & (tiles_e[None, :] > 0)
    next_e = jnp.min(jnp.where(later, experts[None, :], N_EXPERTS), axis=1)
    next_e = jnp.where(next_e == N_EXPERTS, -1, next_e)
    as_i32 = lambda a: a.astype(jnp.int32)
    return (as_i32(tile), as_i32(e), as_i32(lo), as_i32(hi), as_i32(tile != prev_tile),
            as_i32(e != prev_e), as_i32(pick(next_e)), as_i32(total).reshape(1))


def _combine_kernel(tokens, pos_ref, x1_ref, meta_ref, p_ref, ys_hbm,
                    g_ple_ref, w_pg_ref, w_ple_ref, g_final_ref, y_ref, rows, sem_rows):
    i, n = pl.program_id(0), pl.num_programs(0)

    def gather_block(block, slot):
        def copy(t, k):
            src_row = pos_ref[(block * tokens + t) * TOP_K + k]
            return _row_copy(ys_hbm, src_row, rows.at[slot, k], t, sem_rows.at[slot])

        _for_each_row(tokens, lambda t, k: copy(t, k).start(priority=k % 2))

    @pl.when(i == 0)
    def _():
        gather_block(0, 0)

    @pl.when(i + 1 < n)
    def _():
        gather_block(i + 1, (i + 1) % 2)

    slot = i % 2
    _for_each_row(tokens, lambda t, k: _row_copy(ys_hbm, 0, rows.at[slot, 0], 0, sem_rows.at[slot]).wait())

    meta = meta_ref[...]
    x1 = x1_ref[...]
    gates = [jnp.broadcast_to(meta[:, META_GATE + k:META_GATE + k + 1], (tokens, LANES)) for k in range(TOP_K)]
    cols = []
    for c in range(TOKEN_TILE):
        acc = x1[:, c * LANES:(c + 1) * LANES]
        for k in range(TOP_K):
            acc = acc + gates[k] * rows[slot, k, pl.ds(c, tokens, stride=TOKEN_TILE), :]
        cols.append(acc)
    x2 = jnp.concatenate(cols, axis=1)
    hp = _rms(x2, g_ple_ref[...]).astype(BF16)
    gate = jax.nn.sigmoid(_dot(hp, w_pg_ref[...]))
    x3 = x2 + _dot(p_ref[...].astype(BF16), w_ple_ref[...]) * gate
    y_ref[...] = _rms(x3, g_final_ref[...])


def _combine(tokens, pos, x1, meta, p, ys, wts):
    n_out = x1.shape[0]
    tok = lambda w: pl.BlockSpec((tokens, w), lambda i, *_: (i, 0))
    grid_spec = pltpu.PrefetchScalarGridSpec(
        num_scalar_prefetch=1,
        grid=(n_out // tokens,),
        in_specs=[tok(D_MODEL), tok(LANES), tok(PLE_DIM), pl.BlockSpec(memory_space=pl.ANY)]
                 + [pl.BlockSpec(w.shape, lambda i, *_: (0, 0)) for w in wts],
        out_specs=tok(D_MODEL),
        scratch_shapes=[pltpu.VMEM((2, TOP_K, tokens * TOKEN_TILE, LANES), F32),
                        pltpu.SemaphoreType.DMA((2,))],
    )
    return pl.pallas_call(
        functools.partial(_combine_kernel, tokens),
        grid_spec=grid_spec,
        out_shape=jax.ShapeDtypeStruct((n_out, D_MODEL), F32),
        compiler_params=pltpu.CompilerParams(
            dimension_semantics=("arbitrary",), vmem_limit_bytes=VMEM_LIMIT),
        name="combine_ple",
    )(pos, x1, meta, p, ys, *wts)


def _rope_tables(pos, n_rows):
    out = []
    for dim in (ATT_HEAD_DIM, RET_DK):
        half = dim // 2
        inv = ROPE_THETA ** (-np.arange(half, dtype=np.float64) / half)
        ang = np.asarray(pos, np.float64)[:, None] * inv[None, :]
        cos, sin = np.cos(ang), np.sin(ang)
        reps = LANES // dim
        cos_row = np.tile(np.concatenate([cos, cos], axis=-1), (1, reps))
        sin_row = np.tile(np.concatenate([-sin, sin], axis=-1), (1, reps))
        out += [jnp.asarray(np.broadcast_to(cos_row, (n_rows, LANES)), F32),
                jnp.asarray(np.broadcast_to(sin_row, (n_rows, LANES)), F32)]
    return out


def _retention_tables():
    log_g = np.log(1.0 - 2.0 ** (-5.0 - np.arange(RET_HEADS, dtype=np.float64)))
    t = np.arange(BLK, dtype=np.float64)
    rel = t[:, None] - t[None, :]
    intra = np.where(rel >= 0, np.exp(np.maximum(rel, 0.0)[None] * log_g[:, None, None]), 0.0)
    cross = np.exp((t + 1.0)[None, :] * log_g[:, None])
    kdec = np.exp((BLK - 1.0 - t)[None, :] * log_g[:, None])
    lane_bcast = lambda a: np.broadcast_to(a[:, :, None], (RET_HEADS, BLK, LANES))
    tabs = tuple(jnp.asarray(a, F32) for a in (intra, lane_bcast(cross), lane_bcast(kdec)))
    return log_g, tabs


def kernel(x_prompt, x_sample, state_swa_k, state_swa_v, state_ret, p_prompt, p_sample, g_mix, w_in, b_in,
           attn_sinks, g_ret_norm, w_out, b_out, g_mlp, w_router, b_router, w_e_in, b_e_in, w_e_out,
           b_e_out, g_ple, w_ple_gate, w_ple, g_final):
    bsz, seq, _ = x_prompt.shape
    n_dec = x_sample.shape[0]
    n_prompt = bsz * seq
    assert x_sample.shape[1] == 1 and g_mix.shape[0] == 1
    assert state_swa_k.shape[2] == WINDOW and n_dec <= DISPATCH_TOKENS

    row = lambda a: a.reshape(1, -1)
    log_g, ret_tabs = _retention_tables()
    sinks = attn_sinks[0].astype(F32)
    scal_prompt = jnp.concatenate([sinks, jnp.asarray(np.exp(BLK * log_g), F32)])
    scal_sample = jnp.concatenate([sinks, jnp.asarray(np.exp(log_g), F32)])

    w_in_bf = w_in[0].astype(BF16)
    w_out_bf = w_out[0].astype(BF16)
    wr = jnp.pad(w_router[0], ((0, 0), (0, LANES - N_EXPERTS)))
    wr_hi = wr.astype(BF16)
    wr_lo = (wr - wr_hi.astype(F32)).astype(BF16)
    br = jnp.pad(b_router[0], (0, LANES - N_EXPERTS)).reshape(1, LANES)
    post_wts = (row(g_ret_norm[0]), w_out_bf, row(b_out[0]), row(g_mlp[0]), wr_hi, wr_lo, br)

    tabs_p = _rope_tables(np.arange(seq), seq)
    x1_p, h2_p, meta_p, cnt, kp, vp, st_p = _mixer_prompt(
        scal_prompt, x_prompt, tabs_p, ret_tabs, (row(g_mix[0]), w_in_bf, row(b_in[0])) + post_wts)

    tabs_s = _rope_tables(np.full((1,), PAST_LEN), n_dec)
    xs2 = x_sample.reshape(n_dec, D_MODEL)
    aq, ak, av, rq, rk, rv, rg = _sample_inproj(xs2, tabs_s, row(g_mix[0]), w_in_bf, row(b_in[0]))
    att, ret, ks, vs, st_s = _sample_state(
        scal_sample, aq, ak, av, rq, rk, rv,
        state_swa_k[0].reshape(n_dec, WINDOW, LANES), state_swa_v[0].reshape(n_dec, WINDOW, LANES),
        state_ret[0])
    x1_s, h2_s, meta_s, cnt = _sample_post(xs2, att, ret, rg, cnt, post_wts)

    counts = cnt[0, :N_EXPERTS].astype(jnp.int32)
    starts = jnp.cumsum(counts) - counts
    def sorted_rows(meta):
        sel = meta[:, META_IDX:META_IDX + TOP_K].astype(jnp.int32)
        rank = meta[:, META_RANK:META_RANK + TOP_K].astype(jnp.int32)
        is_e = sel[:, :, None] == jnp.arange(N_EXPERTS, dtype=jnp.int32)[None, None, :]
        return (jnp.sum(jnp.where(is_e, starts[None, None, :], 0), axis=-1) + rank).reshape(-1)

    pos_p, pos_s = sorted_rows(meta_p), sorted_rows(meta_s)

    xs_sorted = _dispatch(pos_p, pos_s, h2_p, h2_s)
    visits = _visit_schedule(counts, (n_prompt + n_dec) * TOP_K)
    ys_sorted = _experts(visits, xs_sorted, w_e_in[0], b_e_in[0], w_e_out[0], b_e_out[0])

    ple_wts = (row(g_ple[0]), w_ple_gate[0].astype(BF16), w_ple[0].astype(BF16), row(g_final))
    y_p = _combine(COMBINE_TOKENS, pos_p, x1_p, meta_p, p_prompt[0].reshape(n_prompt, PLE_DIM),
                   ys_sorted, ple_wts)
    y_s = _combine(n_dec, pos_s, x1_s, meta_s, p_sample[0].reshape(n_dec, PLE_DIM), ys_sorted, ple_wts)

    kv_shape = (1, bsz, WINDOW, ATT_KV_HEADS, ATT_HEAD_DIM)
    dec_shape = (1, n_dec, WINDOW, ATT_KV_HEADS, ATT_HEAD_DIM)
    return (y_p.reshape(bsz, seq, D_MODEL), y_s.reshape(n_dec, 1, D_MODEL),
            kp.reshape(kv_shape), vp.reshape(kv_shape), st_p[None],
            ks.reshape(dec_shape), vs.reshape(dec_shape), st_s[None])
```

```python
import functools

import jax
import jax.numpy as jnp
import numpy as np
from jax import lax
from jax.experimental import pallas as pl
from jax.experimental.pallas import tpu as pltpu

D_MODEL = 1024
PAST_LEN = 16384
PLE_DIM = 256
ATT_HEADS = 8
ATT_KV_HEADS = 2
ATT_HEAD_DIM = 64
ATT_GROUP = ATT_HEADS // ATT_KV_HEADS
WINDOW = 128
BLK = 128
RET_HEADS = 4
RET_DK = 128
RET_DV = 128
ROPE_THETA = 10000.0
N_EXPERTS = 32
TOP_K = 4
D_FF = 1024
SWIGLU_LIMIT = 7.0
SWIGLU_ALPHA = 1.702
NORM_EPS = 1e-5
GN_EPS = 1e-6
NEG_INF = -1e30

ATT_Q_W = ATT_HEADS * ATT_HEAD_DIM
ATT_KV_W = ATT_KV_HEADS * ATT_HEAD_DIM
RET_W = RET_HEADS * RET_DK
IN_W = ATT_Q_W + 2 * ATT_KV_W + 4 * RET_W
O_AQ = 0
O_AK = ATT_Q_W
O_AV = O_AK + ATT_KV_W
O_RQ = O_AV + ATT_KV_W
O_RK = O_RQ + RET_W
O_RV = O_RK + RET_W
O_RG = O_RV + RET_W

LANES = 128
META_IDX, META_RANK, META_GATE = 0, 4, 8

MIX_TOKENS = 256
DISPATCH_TOKENS = 256
EXPERT_ROWS = 256
COMBINE_TOKENS = 256
SAMPLE_CHUNK = 16
ROW_UNROLL = 4
VMEM_LIMIT = 56 * 1024 * 1024

F32 = jnp.float32
BF16 = jnp.bfloat16


def _dot(a, b):
    return jnp.dot(a, b, preferred_element_type=F32)


def _dot_nt(a, b):
    return lax.dot_general(a, b, (((1,), (1,)), ((), ())), preferred_element_type=F32)


def _dot_tn(a, b):
    return lax.dot_general(a, b, (((0,), (0,)), ((), ())), preferred_element_type=F32)


TOKEN_SLABS = 2
TOKEN_ROWS = D_MODEL // (TOKEN_SLABS * LANES)


def _token_shape(n_tokens):
    return (TOKEN_SLABS, n_tokens * TOKEN_ROWS, LANES)


def _token_chunks():
    for c in range(TOKEN_ROWS):
        for j in range(TOKEN_SLABS):
            yield j, c, slice((TOKEN_SLABS * c + j) * LANES, (TOKEN_SLABS * c + j + 1) * LANES)


def _store_token_tiles(ref, first_token, rows):
    n = rows.shape[0]
    for j, c, cols in _token_chunks():
        ref[j, pl.ds(first_token * TOKEN_ROWS + c, n, stride=TOKEN_ROWS), :] = rows[:, cols]


def _load_token_tiles(ref, n):
    return jnp.concatenate(
        [ref[j, pl.ds(c, n, stride=TOKEN_ROWS), :] for j, c, _ in _token_chunks()], axis=1)


def _rms(x, g):
    return x * lax.rsqrt(jnp.mean(x * x, axis=-1, keepdims=True) + NORM_EPS) * g


def _rope(col, cos, sin_signed, half):
    lane = lax.broadcasted_iota(jnp.int32, col.shape, 1)
    fwd = pltpu.roll(col, LANES - half, 1)
    bwd = pltpu.roll(col, half, 1)
    swapped = jnp.where((lane % (2 * half)) < half, fwd, bwd)
    return col * cos + swapped * sin_signed


def _in_proj(x, g_mix, w_in, b_in, cos64, sin64, cos128, sin128):
    h = _rms(x, g_mix).astype(BF16)
    z = _dot(h, w_in) + b_in

    def cols(off, width):
        return [z[:, off + c * LANES: off + (c + 1) * LANES] for c in range(width // LANES)]

    scale_q = ATT_HEAD_DIM ** -0.5
    aq = [_rope(c, cos64, sin64, ATT_HEAD_DIM // 2) * scale_q for c in cols(O_AQ, ATT_Q_W)]
    ak = _rope(z[:, O_AK:O_AK + LANES], cos64, sin64, ATT_HEAD_DIM // 2)
    av = z[:, O_AV:O_AV + LANES]
    rq = [_rope(c, cos128, sin128, RET_DK // 2) for c in cols(O_RQ, RET_W)]
    rk = [_rope(c, cos128, sin128, RET_DK // 2) * (RET_DK ** -0.5) for c in cols(O_RK, RET_W)]
    rv = cols(O_RV, RET_W)
    rg = z[:, O_RG:O_RG + RET_W]
    return aq, ak, av, rq, rk, rv, rg


def _expand_q(aq_cols, rows):
    lane = lax.broadcasted_iota(jnp.int32, (rows, LANES), 1)
    low = lane < ATT_HEAD_DIM
    pieces = []
    for h in range(ATT_HEADS):
        col, half, kv = aq_cols[h // 2], h % 2, h // ATT_GROUP
        if half != kv:
            col = pltpu.roll(col, ATT_HEAD_DIM, 1)
        pieces.append(jnp.where(low if kv == 0 else jnp.logical_not(low), col, 0.0))
    return jnp.concatenate(pieces, axis=0)


def _collect_heads(o, rows):
    lane = lax.broadcasted_iota(jnp.int32, (rows, LANES), 1)
    low = lane < ATT_HEAD_DIM
    out = []
    for c in range(ATT_HEADS // 2):
        pieces = []
        for half in (0, 1):
            h = 2 * c + half
            oh = o[h * rows:(h + 1) * rows]
            if h // ATT_GROUP != half:
                oh = pltpu.roll(oh, ATT_HEAD_DIM, 1)
            pieces.append(oh)
        out.append(jnp.where(low, pieces[0], pieces[1]))
    return jnp.concatenate(out, axis=1)


def _sink_softmax(s, mask, sink):
    s = jnp.where(mask, s, NEG_INF)
    m = jnp.maximum(jnp.max(s, axis=-1, keepdims=True), sink)
    e = jnp.exp(s - m)
    den = jnp.sum(e, axis=-1, keepdims=True) + jnp.exp(sink - m)
    return e * (1.0 / den)


def _group_norm_gate(ret, rg, g_ret):
    cols = []
    for h in range(RET_HEADS):
        of = ret[:, h * RET_DV:(h + 1) * RET_DV]
        d = of - jnp.mean(of, axis=-1, keepdims=True)
        var = jnp.mean(d * d, axis=-1, keepdims=True)
        cols.append(d * lax.rsqrt(var + GN_EPS))
    y = jnp.concatenate(cols, axis=1) * g_ret
    return y * (rg * jax.nn.sigmoid(rg))


def _route(x1, g_mlp, wr_hi, wr_lo, b_router, cnt):
    n = x1.shape[0]
    h2 = _rms(x1, g_mlp)
    hi = h2.astype(BF16)
    lo = (h2 - hi.astype(F32)).astype(BF16)
    logits = _dot(hi, wr_hi) + (_dot(hi, wr_lo) + _dot(lo, wr_hi)) + b_router
    work = logits.T[:N_EXPERTS]
    expert = lax.broadcasted_iota(jnp.int32, (N_EXPERTS, n), 0)
    idxs, vals = [], []
    for _ in range(TOP_K):
        m = jnp.max(work, axis=0, keepdims=True)
        idx = jnp.min(jnp.where(work == m, expert, N_EXPERTS), axis=0, keepdims=True)
        idxs.append(idx)
        vals.append(m)
        work = jnp.where(expert == idx, -jnp.inf, work)
    es = [jnp.exp(v - vals[0]) for v in vals]
    den = es[0] + es[1] + es[2] + es[3]
    gates = [e / den for e in es]

    sel = (expert == idxs[0]) | (expert == idxs[1]) | (expert == idxs[2]) | (expert == idxs[3])
    onehot = jnp.where(sel, 1.0, 0.0)
    r = lax.broadcasted_iota(jnp.int32, (n, n), 0)
    c = lax.broadcasted_iota(jnp.int32, (n, n), 1)
    earlier = jnp.where(r < c, 1.0, 0.0).astype(BF16)
    before = _dot(onehot.astype(BF16), earlier) + cnt
    row = lax.broadcasted_iota(jnp.int32, (LANES, n), 0)
    meta_t = jnp.zeros((LANES, n), F32)
    for k in range(TOP_K):
        rank = jnp.sum(jnp.where(expert == idxs[k], before, 0.0), axis=0, keepdims=True)
        meta_t = jnp.where(row == META_IDX + k, idxs[k].astype(F32), meta_t)
        meta_t = jnp.where(row == META_RANK + k, rank, meta_t)
        meta_t = jnp.where(row == META_GATE + k, gates[k], meta_t)
    return h2, meta_t.T, cnt + jnp.sum(onehot, axis=1, keepdims=True)


def _mixer_prompt_kernel(scal_ref, x_ref, cos64_ref, sin64_ref, cos128_ref, sin128_ref,
                         intra_ref, cross_ref, kdec_ref,
                         g_mix_ref, w_in_ref, b_in_ref, g_ret_ref, w_out_ref, b_out_ref,
                         g_mlp_ref, wr_hi_ref, wr_lo_ref, b_router_ref,
                         x1_ref, h2_ref, meta_ref, cnt_out_ref, kp_ref, vp_ref, stp_ref,
                         prevk, prevv, st, cnt):
    b, n = pl.program_id(0), pl.program_id(1)

    @pl.when((b == 0) & (n == 0))
    def _():
        cnt[...] = jnp.zeros_like(cnt)

    @pl.when(n == 0)
    def _():
        prevk[...] = jnp.zeros_like(prevk)
        prevv[...] = jnp.zeros_like(prevv)
        st[...] = jnp.zeros_like(st)

    x = x_ref[...]
    aq, ak, av, rq, rk, rv, rg = _in_proj(
        x, g_mix_ref[...], w_in_ref[...], b_in_ref[...],
        cos64_ref[...], sin64_ref[...], cos128_ref[...], sin128_ref[...])

    row = lax.broadcasted_iota(jnp.int32, (BLK, 2 * BLK), 0)
    col = lax.broadcasted_iota(jnp.int32, (BLK, 2 * BLK), 1)
    mask_prev = (col < BLK) & (col > row)
    mask_cur = (col >= BLK) & (col - BLK <= row)

    k_prev, v_prev = prevk[...], prevv[...]
    mixed = []
    for j in range(MIX_TOKENS // BLK):
        sl = slice(j * BLK, (j + 1) * BLK)
        kk = jnp.concatenate([k_prev, ak[sl]], axis=0).astype(BF16)
        vv = jnp.concatenate([v_prev, av[sl]], axis=0).astype(BF16)
        qall = _expand_q([cq[sl] for cq in aq], BLK).astype(BF16)
        s = _dot_nt(qall, kk)
        mask = mask_cur | mask_prev if j > 0 else mask_cur | (mask_prev & (n > 0))
        probs = [_sink_softmax(s[h * BLK:(h + 1) * BLK], mask, scal_ref[h]) for h in range(ATT_HEADS)]
        o = _dot(jnp.concatenate(probs, axis=0).astype(BF16), vv)
        att = _collect_heads(o, BLK)
        k_prev, v_prev = ak[sl], av[sl]

        ret = []
        for h in range(RET_HEADS):
            qb, kh, vb = rq[h][sl].astype(BF16), rk[h][sl], rv[h][sl].astype(BF16)
            sc = _dot_nt(qb, kh.astype(BF16)) * intra_ref[h]
            sth = st[h]
            o_h = _dot(sc.astype(BF16), vb) + _dot(qb, sth.astype(BF16)) * cross_ref[h]
            st[h] = sth * scal_ref[ATT_HEADS + h] + _dot_tn((kh * kdec_ref[h]).astype(BF16), vb)
            ret.append(o_h)
        ret_out = _group_norm_gate(jnp.concatenate(ret, axis=1), rg[sl], g_ret_ref[...])
        mixed.append(jnp.concatenate([att, ret_out], axis=1))

    mixed = jnp.concatenate(mixed, axis=0).astype(BF16)
    x1 = x + _dot(mixed, w_out_ref[...]) + b_out_ref[...]
    x1_ref[...] = x1

    c = cnt[...]
    for j in range(MIX_TOKENS // BLK):
        sl = slice(j * BLK, (j + 1) * BLK)
        h2, meta, c = _route(x1[sl], g_mlp_ref[...], wr_hi_ref[...], wr_lo_ref[...], b_router_ref[...], c)
        _store_token_tiles(h2_ref, j * BLK, h2)
        meta_ref[sl, :] = meta
    cnt[...] = c
    cnt_out_ref[...] = c

    prevk[...] = k_prev
    prevv[...] = v_prev
    kp_ref[0] = k_prev
    vp_ref[0] = v_prev
    stp_ref[0] = st[...]


def _const_spec(shape):
    nd = len(shape)
    return pl.BlockSpec(shape, lambda *_: (0,) * nd)


def _mixer_prompt(scal, x, tabs, ret_tabs, wts):
    bsz, seq, _ = x.shape
    steps = seq // MIX_TOKENS
    x2 = x.reshape(bsz * seq, D_MODEL)
    tok = lambda w: pl.BlockSpec((MIX_TOKENS, w), lambda b, n: (b * steps + n, 0))
    pos = lambda w: pl.BlockSpec((MIX_TOKENS, w), lambda b, n: (n, 0))
    in_specs = ([pl.BlockSpec(memory_space=pltpu.SMEM), tok(D_MODEL)]
                + [pos(LANES)] * 4
                + [_const_spec(t.shape) for t in ret_tabs]
                + [_const_spec(w.shape) for w in wts])
    out_shape = (
        jax.ShapeDtypeStruct((bsz * seq, D_MODEL), F32),
        jax.ShapeDtypeStruct(_token_shape(bsz * seq), F32),
        jax.ShapeDtypeStruct((bsz * seq, LANES), F32),
        jax.ShapeDtypeStruct((N_EXPERTS, LANES), F32),
        jax.ShapeDtypeStruct((bsz, BLK, LANES), F32),
        jax.ShapeDtypeStruct((bsz, BLK, LANES), F32),
        jax.ShapeDtypeStruct((bsz, RET_HEADS, RET_DK, RET_DV), F32),
    )
    out_specs = (
        tok(D_MODEL), pl.BlockSpec(_token_shape(MIX_TOKENS), lambda b, n: (0, b * steps + n, 0)),
        tok(LANES), _const_spec((N_EXPERTS, LANES)),
        pl.BlockSpec((1, BLK, LANES), lambda b, n: (b, 0, 0)),
        pl.BlockSpec((1, BLK, LANES), lambda b, n: (b, 0, 0)),
        pl.BlockSpec((1, RET_HEADS, RET_DK, RET_DV), lambda b, n: (b, 0, 0, 0)),
    )
    return pl.pallas_call(
        _mixer_prompt_kernel,
        grid=(bsz, steps),
        in_specs=in_specs,
        out_specs=out_specs,
        out_shape=out_shape,
        scratch_shapes=[pltpu.VMEM((BLK, LANES), F32), pltpu.VMEM((BLK, LANES), F32),
                        pltpu.VMEM((RET_HEADS, RET_DK, RET_DV), F32), pltpu.VMEM((N_EXPERTS, LANES), F32)],
        compiler_params=pltpu.CompilerParams(
            dimension_semantics=("arbitrary", "arbitrary"), vmem_limit_bytes=VMEM_LIMIT),
        name="mixer_prompt",
    )(scal, x2, *tabs, *ret_tabs, *wts)


def _sample_inproj_kernel(x_ref, cos64_ref, sin64_ref, cos128_ref, sin128_ref,
                          g_mix_ref, w_in_ref, b_in_ref,
                          aq_ref, ak_ref, av_ref, rq_ref, rk_ref, rv_ref, rg_ref):
    aq, ak, av, rq, rk, rv, rg = _in_proj(
        x_ref[...], g_mix_ref[...], w_in_ref[...], b_in_ref[...],
        cos64_ref[...], sin64_ref[...], cos128_ref[...], sin128_ref[...])
    aq_ref[...] = jnp.concatenate(aq, axis=1)
    ak_ref[...] = ak
    av_ref[...] = av
    rq_ref[...] = jnp.concatenate(rq, axis=1)
    rk_ref[...] = jnp.concatenate(rk, axis=1)
    rv_ref[...] = jnp.concatenate(rv, axis=1)
    rg_ref[...] = rg


def _sample_inproj(x, tabs, g_mix, w_in, b_in):
    n = x.shape[0]
    widths = (ATT_Q_W, LANES, LANES, RET_W, RET_W, RET_W, RET_W)
    args = (x, *tabs, g_mix, w_in, b_in)
    return pl.pallas_call(
        _sample_inproj_kernel,
        grid=(1,),
        in_specs=[_const_spec(a.shape) for a in args],
        out_specs=tuple(_const_spec((n, w)) for w in widths),
        out_shape=tuple(jax.ShapeDtypeStruct((n, w), F32) for w in widths),
        compiler_params=pltpu.CompilerParams(vmem_limit_bytes=VMEM_LIMIT),
        name="sample_inproj",
    )(*args)


def _sample_state_kernel(scal_ref, aq_ref, ak_ref, av_ref, rq_ref, rk_ref, rv_ref,
                         kbuf_ref, vbuf_ref, st_ref,
                         att_ref, ret_ref, kout_ref, vout_ref, stout_ref):
    nb = SAMPLE_CHUNK
    rows = nb * WINDOW

    def shift_in(buf_ref, new):
        flat = buf_ref[...].reshape(rows, LANES)
        rolled = pltpu.roll(flat, rows - 1, 0)
        rolled = rolled.reshape(nb, WINDOW, LANES)
        pos = lax.broadcasted_iota(jnp.int32, (nb, WINDOW, LANES), 1)
        return jnp.where(pos == WINDOW - 1, new[:, None, :], rolled)

    knew = shift_in(kbuf_ref, ak_ref[...])
    vnew = shift_in(vbuf_ref, av_ref[...])
    kout_ref[...] = knew
    vout_ref[...] = vnew

    aq = aq_ref[...]
    qall = _expand_q([aq[:, c * LANES:(c + 1) * LANES] for c in range(ATT_Q_W // LANES)], nb)
    s = _dot_nt(qall.astype(BF16), knew.reshape(rows, LANES).astype(BF16))
    r = lax.broadcasted_iota(jnp.int32, (nb, rows), 0)
    c = lax.broadcasted_iota(jnp.int32, (nb, rows), 1)
    own = (c // WINDOW) == r
    probs = [_sink_softmax(s[h * nb:(h + 1) * nb], own, scal_ref[h]) for h in range(ATT_HEADS)]
    o = _dot(jnp.concatenate(probs, axis=0).astype(BF16), vnew.reshape(rows, LANES).astype(BF16))
    att_ref[...] = _collect_heads(o, nb)

    qt = rq_ref[...].T
    kt = rk_ref[...].T
    rv = rv_ref[...]
    for bi in range(nb):
        for h in range(RET_HEADS):
            hs = slice(h * RET_DK, (h + 1) * RET_DK)
            s_new = st_ref[bi, h] * scal_ref[ATT_HEADS + h] + kt[hs, bi:bi + 1] * rv[bi:bi + 1, hs]
            stout_ref[bi, h] = s_new
            ret_ref[bi:bi + 1, hs] = jnp.sum(qt[hs, bi:bi + 1] * s_new, axis=0, keepdims=True)


def _sample_state(scal, aq, ak, av, rq, rk, rv, kbuf, vbuf, st):
    n = aq.shape[0]
    nb = SAMPLE_CHUNK
    row = lambda w: pl.BlockSpec((nb, w), lambda i: (i, 0))
    buf = pl.BlockSpec((nb, WINDOW, LANES), lambda i: (i, 0, 0))
    stt = pl.BlockSpec((nb, RET_HEADS, RET_DK, RET_DV), lambda i: (i, 0, 0, 0))
    return pl.pallas_call(
        _sample_state_kernel,
        grid=(n // nb,),
        in_specs=[pl.BlockSpec(memory_space=pltpu.SMEM), row(ATT_Q_W), row(LANES), row(LANES),
                  row(RET_W), row(RET_W), row(RET_W), buf, buf, stt],
        out_specs=(row(ATT_Q_W), row(RET_W), buf, buf, stt),
        out_shape=(jax.ShapeDtypeStruct((n, ATT_Q_W), F32), jax.ShapeDtypeStruct((n, RET_W), F32),
                   jax.ShapeDtypeStruct(kbuf.shape, F32), jax.ShapeDtypeStruct(vbuf.shape, F32),
                   jax.ShapeDtypeStruct(st.shape, F32)),
        compiler_params=pltpu.CompilerParams(
            dimension_semantics=("arbitrary",), vmem_limit_bytes=VMEM_LIMIT),
        name="sample_state",
    )(scal, aq, ak, av, rq, rk, rv, kbuf, vbuf, st)


def _sample_post_kernel(x_ref, att_ref, ret_ref, rg_ref, cnt_in_ref,
                        g_ret_ref, w_out_ref, b_out_ref, g_mlp_ref, wr_hi_ref, wr_lo_ref, b_router_ref,
                        x1_ref, h2_ref, meta_ref, cnt_out_ref):
    x = x_ref[...]
    ret_out = _group_norm_gate(ret_ref[...], rg_ref[...], g_ret_ref[...])
    mixed = jnp.concatenate([att_ref[...], ret_out], axis=1).astype(BF16)
    x1 = x + _dot(mixed, w_out_ref[...]) + b_out_ref[...]
    x1_ref[...] = x1
    h2, meta, c = _route(x1, g_mlp_ref[...], wr_hi_ref[...], wr_lo_ref[...], b_router_ref[...],
                         cnt_in_ref[...])
    _store_token_tiles(h2_ref, 0, h2)
    meta_ref[...] = meta
    cnt_out_ref[...] = c


def _sample_post(x, att, ret, rg, cnt, wts):
    n = x.shape[0]
    vm = [x, att, ret, rg, cnt, *wts]
    shapes = ((n, D_MODEL), _token_shape(n), (n, LANES), (N_EXPERTS, LANES))
    return pl.pallas_call(
        _sample_post_kernel,
        grid=(1,),
        in_specs=[_const_spec(a.shape) for a in vm],
        out_specs=tuple(_const_spec(s) for s in shapes),
        out_shape=tuple(jax.ShapeDtypeStruct(s, F32) for s in shapes),
        compiler_params=pltpu.CompilerParams(vmem_limit_bytes=VMEM_LIMIT),
        name="sample_post",
    )(*vm)


def _row_copy(src, src_token, dst, dst_token, sem):
    rows = lambda tok: pl.ds(pl.multiple_of(tok * TOKEN_ROWS, TOKEN_ROWS), TOKEN_ROWS)
    return pltpu.make_async_copy(src.at[:, rows(src_token)], dst.at[:, rows(dst_token)], sem)


def _for_each_row(n_tok, fn):
    def body(g, carry):
        for u in range(ROW_UNROLL):
            for k in range(TOP_K):
                fn(g * ROW_UNROLL + u, k)
        return carry

    lax.fori_loop(0, n_tok // ROW_UNROLL, body, 0)


def _dispatch_kernel(prompt_steps, n_dec, pos_p, pos_s, h2p_ref, h2s_ref, xs_hbm, sem_rows):
    i = pl.program_id(0)

    def scatter_rows(src_ref, n_tok, pos_ref, first_idx):
        def copy(t, k):
            return _row_copy(src_ref, t, xs_hbm, pos_ref[first_idx + t * TOP_K + k], sem_rows)

        _for_each_row(n_tok, lambda t, k: copy(t, k).start(priority=k % 2))
        _for_each_row(n_tok, lambda t, k: _row_copy(src_ref, 0, xs_hbm, 0, sem_rows).wait())

    @pl.when(i < prompt_steps)
    def _():
        scatter_rows(h2p_ref, DISPATCH_TOKENS, pos_p, i * (DISPATCH_TOKENS * TOP_K))

    @pl.when(i == prompt_steps)
    def _():
        scatter_rows(h2s_ref, n_dec, pos_s, 0)


def _dispatch(pos_p, pos_s, h2_p, h2_s):
    n_prompt, n_dec = h2_p.shape[1] // TOKEN_ROWS, h2_s.shape[1] // TOKEN_ROWS
    prompt_steps = n_prompt // DISPATCH_TOKENS
    grid_spec = pltpu.PrefetchScalarGridSpec(
        num_scalar_prefetch=2,
        grid=(prompt_steps + 1,),
        in_specs=[pl.BlockSpec(_token_shape(DISPATCH_TOKENS),
                               lambda i, *_: (0, jnp.minimum(i, prompt_steps - 1), 0)),
                  pl.BlockSpec(h2_s.shape, lambda i, *_: (0, 0, 0))],
        out_specs=pl.BlockSpec(memory_space=pl.ANY),
        scratch_shapes=[pltpu.SemaphoreType.DMA],
    )
    return pl.pallas_call(
        functools.partial(_dispatch_kernel, prompt_steps, n_dec),
        grid_spec=grid_spec,
        out_shape=jax.ShapeDtypeStruct(_token_shape((n_prompt + n_dec) * TOP_K), F32),
        compiler_params=pltpu.CompilerParams(
            dimension_semantics=("arbitrary",), vmem_limit_bytes=VMEM_LIMIT),
        name="dispatch",
    )(pos_p, pos_s, h2_p, h2_s)


def _experts_kernel(v_tile, v_expert, v_lo, v_hi, v_newtile, v_newexp, v_next, n_visits,
                    xs_ref, w_in_hbm, b_in_ref, w_out_hbm, b_out_ref, y_ref,
                    w_in_f32, w_out_f32, w_in_bf, w_out_bf, sems):
    v = pl.program_id(0)

    def weight_copies(e):
        return (pltpu.make_async_copy(w_in_hbm.at[e], w_in_f32, sems.at[0]),
                pltpu.make_async_copy(w_out_hbm.at[e], w_out_f32, sems.at[1]))

    @pl.when(v < n_visits[0])
    def _():
        @pl.when(v_newexp[v] == 1)
        def _():
            @pl.when(v == 0)
            def _():
                for cp in weight_copies(v_expert[0]):
                    cp.start(priority=1)

            for cp in weight_copies(v_expert[v]):
                cp.wait()
            w_in_bf[...] = w_in_f32[...].astype(BF16)
            w_out_bf[...] = w_out_f32[...].astype(BF16)

            @pl.when(v_next[v] >= 0)
            def _():
                for cp in weight_copies(v_next[v]):
                    cp.start(priority=1)

        x = _load_token_tiles(xs_ref, EXPERT_ROWS).astype(BF16)
        hg = _dot(x, w_in_bf[...]) + b_in_ref[0]
        gate = jnp.minimum(hg[:, :D_FF], SWIGLU_LIMIT)
        lin = jnp.clip(hg[:, D_FF:], -SWIGLU_LIMIT, SWIGLU_LIMIT)
        act = gate * jax.nn.sigmoid(SWIGLU_ALPHA * gate) * (lin + 1.0)
        y = _dot(act.astype(BF16), w_out_bf[...]) + b_out_ref[0]

        row = lax.broadcasted_iota(jnp.int32, (EXPERT_ROWS, LANES), 0)
        mine = (row >= v_lo[v]) & (row < v_hi[v])

        def write_rows(first_visit_of_tile):
            for j, c, cols in _token_chunks():
                rows_c = pl.ds(c, EXPERT_ROWS, stride=TOKEN_ROWS)
                keep = 0.0 if first_visit_of_tile else y_ref[j, rows_c, :]
                y_ref[j, rows_c, :] = jnp.where(mine, y[:, cols], keep)

        @pl.when(v_newtile[v] == 1)
        def _():
            write_rows(True)

        @pl.when(v_newtile[v] == 0)
        def _():
            write_rows(False)


def _experts(visits, xs, w_e_in, b_e_in, w_e_out, b_e_out):
    n_rows = xs.shape[1] // TOKEN_ROWS
    max_visits = n_rows // EXPERT_ROWS + N_EXPERTS - 1
    tile_rows = pl.BlockSpec(_token_shape(EXPERT_ROWS), lambda v, vt, *_: (0, vt[v], 0))
    any_spec = pl.BlockSpec(memory_space=pl.ANY)
    grid_spec = pltpu.PrefetchScalarGridSpec(
        num_scalar_prefetch=len(visits),
        grid=(max_visits,),
        in_specs=[
            tile_rows, any_spec,
            pl.BlockSpec((1, 1, 2 * D_FF), lambda v, vt, ve, *_: (ve[v], 0, 0)),
            any_spec,
            pl.BlockSpec((1, 1, D_MODEL), lambda v, vt, ve, *_: (ve[v], 0, 0)),
        ],
        out_specs=tile_rows,
        scratch_shapes=[pltpu.VMEM((D_MODEL, 2 * D_FF), F32), pltpu.VMEM((D_FF, D_MODEL), F32),
                        pltpu.VMEM((D_MODEL, 2 * D_FF), BF16), pltpu.VMEM((D_FF, D_MODEL), BF16),
                        pltpu.SemaphoreType.DMA((2,))],
    )
    return pl.pallas_call(
        _experts_kernel,
        grid_spec=grid_spec,
        out_shape=jax.ShapeDtypeStruct(xs.shape, F32),
        compiler_params=pltpu.CompilerParams(
            dimension_semantics=("arbitrary",), vmem_limit_bytes=VMEM_LIMIT),
        name="experts",
    )(*visits, xs, w_e_in, b_e_in.reshape(N_EXPERTS, 1, 2 * D_FF), w_e_out,
      b_e_out.reshape(N_EXPERTS, 1, D_MODEL))


def _visit_schedule(counts, n_rows):
    n_tiles = n_rows // EXPERT_ROWS
    max_visits = n_tiles + N_EXPERTS - 1
    ends = jnp.cumsum(counts)
    starts = ends - counts
    first_tile = starts // EXPERT_ROWS
    tiles_e = jnp.where(counts > 0, (ends - 1) // EXPERT_ROWS - first_tile + 1, 0)
    vis_end = jnp.cumsum(tiles_e)
    vis_start = vis_end - tiles_e
    total = vis_end[-1]
    v = jnp.minimum(jnp.arange(max_visits, dtype=jnp.int32), total - 1)
    e = jnp.sum((vis_end[None, :] <= v[:, None]).astype(jnp.int32), axis=1)
    is_e = e[:, None] == jnp.arange(N_EXPERTS, dtype=jnp.int32)[None, :]
    pick = lambda table: jnp.sum(jnp.where(is_e, table[None, :], 0), axis=1)
    tile = pick(first_tile) + (v - pick(vis_start))
    lo = jnp.maximum(pick(starts), tile * EXPERT_ROWS) - tile * EXPERT_ROWS
    hi = jnp.minimum(pick(ends), (tile + 1) * EXPERT_ROWS) - tile * EXPERT_ROWS
    prev_tile = jnp.concatenate([jnp.full((1,), -1, jnp.int32), tile[:-1]])
    prev_e = jnp.concatenate([jnp.full((1,), -1, jnp.int32), e[:-1]])
    experts = jnp.arange(N_EXPERTS, dtype=jnp.int32)
    later = (experts[None, :] > experts[:, None]) & (tiles_e[None, :] > 0)
    next_e = jnp.min(jnp.where(later, experts[None, :], N_EXPERTS), axis=1)
    next_e = jnp.where(next_e == N_EXPERTS, -1, next_e)
    as_i32 = lambda a: a.astype(jnp.int32)
    return (as_i32(tile), as_i32(e), as_i32(lo), as_i32(hi), as_i32(tile != prev_tile),
            as_i32(e != prev_e), as_i32(pick(next_e)), as_i32(total).reshape(1))


def _combine_kernel(tokens, pos_ref, x1_ref, meta_ref, p_ref, ys_hbm,
                    g_ple_ref, w_pg_ref, w_ple_ref, g_final_ref, y_ref, rows, sem_rows):
    i, n = pl.program_id(0), pl.num_programs(0)

    def gather_block(block, slot):
        def copy(t, k):
            src_row = pos_ref[(block * tokens + t) * TOP_K + k]
            return _row_copy(ys_hbm, src_row, rows.at[slot, k], t, sem_rows.at[slot])

        _for_each_row(tokens, lambda t, k: copy(t, k).start(priority=k % 2))

    @pl.when(i == 0)
    def _():
        gather_block(0, 0)

    @pl.when(i + 1 < n)
    def _():
        gather_block(i + 1, (i + 1) % 2)

    slot = i % 2
    _for_each_row(tokens, lambda t, k: _row_copy(ys_hbm, 0, rows.at[slot, 0], 0, sem_rows.at[slot]).wait())

    meta = meta_ref[...]
    x1 = x1_ref[...]
    gates = [jnp.broadcast_to(meta[:, META_GATE + k:META_GATE + k + 1], (tokens, LANES)) for k in range(TOP_K)]
    cols = []
    for j, c, xcols in _token_chunks():
        acc = x1[:, xcols]
        for k in range(TOP_K):
            acc = acc + gates[k] * rows[slot, k, j, pl.ds(c, tokens, stride=TOKEN_ROWS), :]
        cols.append(acc)
    x2 = jnp.concatenate(cols, axis=1)
    hp = _rms(x2, g_ple_ref[...]).astype(BF16)
    gate = jax.nn.sigmoid(_dot(hp, w_pg_ref[...]))
    x3 = x2 + _dot(p_ref[...].astype(BF16), w_ple_ref[...]) * gate
    y_ref[...] = _rms(x3, g_final_ref[...])


def _combine(tokens, pos, x1, meta, p, ys, wts):
    n_out = x1.shape[0]
    tok = lambda w: pl.BlockSpec((tokens, w), lambda i, *_: (i, 0))
    grid_spec = pltpu.PrefetchScalarGridSpec(
        num_scalar_prefetch=1,
        grid=(n_out // tokens,),
        in_specs=[tok(D_MODEL), tok(LANES), tok(PLE_DIM), pl.BlockSpec(memory_space=pl.ANY)]
                 + [pl.BlockSpec(w.shape, lambda i, *_: (0, 0)) for w in wts],
        out_specs=tok(D_MODEL),
        scratch_shapes=[pltpu.VMEM((2, TOP_K) + _token_shape(tokens), F32),
                        pltpu.SemaphoreType.DMA((2,))],
    )
    return pl.pallas_call(
        functools.partial(_combine_kernel, tokens),
        grid_spec=grid_spec,
        out_shape=jax.ShapeDtypeStruct((n_out, D_MODEL), F32),
        compiler_params=pltpu.CompilerParams(
            dimension_semantics=("arbitrary",), vmem_limit_bytes=VMEM_LIMIT),
        name="combine_ple",
    )(pos, x1, meta, p, ys, *wts)


def _rope_tables(pos, n_rows):
    out = []
    for dim in (ATT_HEAD_DIM, RET_DK):
        half = dim // 2
        inv = ROPE_THETA ** (-np.arange(half, dtype=np.float64) / half)
        ang = np.asarray(pos, np.float64)[:, None] * inv[None, :]
        cos, sin = np.cos(ang), np.sin(ang)
        reps = LANES // dim
        cos_row = np.tile(np.concatenate([cos, cos], axis=-1), (1, reps))
        sin_row = np.tile(np.concatenate([-sin, sin], axis=-1), (1, reps))
        out += [jnp.asarray(np.broadcast_to(cos_row, (n_rows, LANES)), F32),
                jnp.asarray(np.broadcast_to(sin_row, (n_rows, LANES)), F32)]
    return out


def _retention_tables():
    log_g = np.log(1.0 - 2.0 ** (-5.0 - np.arange(RET_HEADS, dtype=np.float64)))
    t = np.arange(BLK, dtype=np.float64)
    rel = t[:, None] - t[None, :]
    intra = np.where(rel >= 0, np.exp(np.maximum(rel, 0.0)[None] * log_g[:, None, None]), 0.0)
    cross = np.exp((t + 1.0)[None, :] * log_g[:, None])
    kdec = np.exp((BLK - 1.0 - t)[None, :] * log_g[:, None])
    lane_bcast = lambda a: np.broadcast_to(a[:, :, None], (RET_HEADS, BLK, LANES))
    tabs = tuple(jnp.asarray(a, F32) for a in (intra, lane_bcast(cross), lane_bcast(kdec)))
    return log_g, tabs


def kernel(x_prompt, x_sample, state_swa_k, state_swa_v, state_ret, p_prompt, p_sample, g_mix, w_in, b_in,
           attn_sinks, g_ret_norm, w_out, b_out, g_mlp, w_router, b_router, w_e_in, b_e_in, w_e_out,
           b_e_out, g_ple, w_ple_gate, w_ple, g_final):
    bsz, seq, _ = x_prompt.shape
    n_dec = x_sample.shape[0]
    n_prompt = bsz * seq
    assert x_sample.shape[1] == 1 and g_mix.shape[0] == 1
    assert state_swa_k.shape[2] == WINDOW and n_dec <= DISPATCH_TOKENS

    row = lambda a: a.reshape(1, -1)
    log_g, ret_tabs = _retention_tables()
    sinks = attn_sinks[0].astype(F32)
    scal_prompt = jnp.concatenate([sinks, jnp.asarray(np.exp(BLK * log_g), F32)])
    scal_sample = jnp.concatenate([sinks, jnp.asarray(np.exp(log_g), F32)])

    w_in_bf = w_in[0].astype(BF16)
    w_out_bf = w_out[0].astype(BF16)
    wr = jnp.pad(w_router[0], ((0, 0), (0, LANES - N_EXPERTS)))
    wr_hi = wr.astype(BF16)
    wr_lo = (wr - wr_hi.astype(F32)).astype(BF16)
    br = jnp.pad(b_router[0], (0, LANES - N_EXPERTS)).reshape(1, LANES)
    post_wts = (row(g_ret_norm[0]), w_out_bf, row(b_out[0]), row(g_mlp[0]), wr_hi, wr_lo, br)

    tabs_p = _rope_tables(np.arange(seq), seq)
    x1_p, h2_p, meta_p, cnt, kp, vp, st_p = _mixer_prompt(
        scal_prompt, x_prompt, tabs_p, ret_tabs, (row(g_mix[0]), w_in_bf, row(b_in[0])) + post_wts)

    tabs_s = _rope_tables(np.full((1,), PAST_LEN), n_dec)
    xs2 = x_sample.reshape(n_dec, D_MODEL)
    aq, ak, av, rq, rk, rv, rg = _sample_inproj(xs2, tabs_s, row(g_mix[0]), w_in_bf, row(b_in[0]))
    att, ret, ks, vs, st_s = _sample_state(
        scal_sample, aq, ak, av, rq, rk, rv,
        state_swa_k[0].reshape(n_dec, WINDOW, LANES), state_swa_v[0].reshape(n_dec, WINDOW, LANES),
        state_ret[0])
    x1_s, h2_s, meta_s, cnt = _sample_post(xs2, att, ret, rg, cnt, post_wts)

    counts = cnt[:, 0].astype(jnp.int32)
    starts = jnp.cumsum(counts) - counts
    def sorted_rows(meta):
        sel = meta[:, META_IDX:META_IDX + TOP_K].astype(jnp.int32)
        rank = meta[:, META_RANK:META_RANK + TOP_K].astype(jnp.int32)
        is_e = sel[:, :, None] == jnp.arange(N_EXPERTS, dtype=jnp.int32)[None, None, :]
        return (jnp.sum(jnp.where(is_e, starts[None, None, :], 0), axis=-1) + rank).reshape(-1)

    pos_p, pos_s = sorted_rows(meta_p), sorted_rows(meta_s)

    xs_sorted = _dispatch(pos_p, pos_s, h2_p, h2_s)
    visits = _visit_schedule(counts, (n_prompt + n_dec) * TOP_K)
    ys_sorted = _experts(visits, xs_sorted, w_e_in[0], b_e_in[0], w_e_out[0], b_e_out[0])

    ple_wts = (row(g_ple[0]), w_ple_gate[0].astype(BF16), w_ple[0].astype(BF16), row(g_final))
    y_p = _combine(COMBINE_TOKENS, pos_p, x1_p, meta_p, p_prompt[0].reshape(n_prompt, PLE_DIM),
                   ys_sorted, ple_wts)
    y_s = _combine(n_dec, pos_s, x1_s, meta_s, p_sample[0].reshape(n_dec, PLE_DIM), ys_sorted, ple_wts)

    kv_shape = (1, bsz, WINDOW, ATT_KV_HEADS, ATT_HEAD_DIM)
    dec_shape = (1, n_dec, WINDOW, ATT_KV_HEADS, ATT_HEAD_DIM)
    return (y_p.reshape(bsz, seq, D_MODEL), y_s.reshape(n_dec, 1, D_MODEL),
            kp.reshape(kv_shape), vp.reshape(kv_shape), st_p[None],
            ks.reshape(dec_shape), vs.reshape(dec_shape), st_s[None])
```

```python
import functools

import jax
import jax.numpy as jnp
import numpy as np
from jax import lax
from jax.experimental import pallas as pl
from jax.experimental.pallas import tpu as pltpu

D_MODEL = 1024
PAST_LEN = 16384
PLE_DIM = 256
ATT_HEADS = 8
ATT_KV_HEADS = 2
ATT_HEAD_DIM = 64
ATT_GROUP = ATT_HEADS // ATT_KV_HEADS
WINDOW = 128
BLK = 128
RET_HEADS = 4
RET_DK = 128
RET_DV = 128
ROPE_THETA = 10000.0
N_EXPERTS = 32
TOP_K = 4
D_FF = 1024
SWIGLU_LIMIT = 7.0
SWIGLU_ALPHA = 1.702
NORM_EPS = 1e-5
GN_EPS = 1e-6
NEG_INF = -1e30

ATT_Q_W = ATT_HEADS * ATT_HEAD_DIM
ATT_KV_W = ATT_KV_HEADS * ATT_HEAD_DIM
RET_W = RET_HEADS * RET_DK
IN_W = ATT_Q_W + 2 * ATT_KV_W + 4 * RET_W
O_AQ = 0
O_AK = ATT_Q_W
O_AV = O_AK + ATT_KV_W
O_RQ = O_AV + ATT_KV_W
O_RK = O_RQ + RET_W
O_RV = O_RK + RET_W
O_RG = O_RV + RET_W

LANES = 128
META_IDX, META_RANK, META_GATE = 0, 4, 8

MIX_TOKENS = 512
DISPATCH_TOKENS = 256
EXPERT_ROWS = 256
COMBINE_TOKENS = 256
SAMPLE_CHUNK = 16
ROW_UNROLL = 4
VMEM_LIMIT = 56 * 1024 * 1024

F32 = jnp.float32
BF16 = jnp.bfloat16


def _dot(a, b):
    return jnp.dot(a, b, preferred_element_type=F32)


def _dot_nt(a, b):
    return lax.dot_general(a, b, (((1,), (1,)), ((), ())), preferred_element_type=F32)


def _dot_tn(a, b):
    return lax.dot_general(a, b, (((0,), (0,)), ((), ())), preferred_element_type=F32)


TOKEN_SLABS = 2
TOKEN_ROWS = D_MODEL // (TOKEN_SLABS * LANES)


def _token_shape(n_tokens):
    return (TOKEN_SLABS, n_tokens * TOKEN_ROWS, LANES)


def _token_chunks():
    for c in range(TOKEN_ROWS):
        for j in range(TOKEN_SLABS):
            yield j, c, slice((TOKEN_SLABS * c + j) * LANES, (TOKEN_SLABS * c + j + 1) * LANES)


def _store_token_tiles(ref, first_token, rows):
    n = rows.shape[0]
    for j, c, cols in _token_chunks():
        ref[j, pl.ds(first_token * TOKEN_ROWS + c, n, stride=TOKEN_ROWS), :] = rows[:, cols]


def _load_token_tiles(ref, n):
    return jnp.concatenate(
        [ref[j, pl.ds(c, n, stride=TOKEN_ROWS), :] for j, c, _ in _token_chunks()], axis=1)


def _rms(x, g):
    return x * lax.rsqrt(jnp.mean(x * x, axis=-1, keepdims=True) + NORM_EPS) * g


def _rope(col, cos, sin_signed, half):
    lane = lax.broadcasted_iota(jnp.int32, col.shape, 1)
    fwd = pltpu.roll(col, LANES - half, 1)
    bwd = pltpu.roll(col, half, 1)
    swapped = jnp.where((lane % (2 * half)) < half, fwd, bwd)
    return col * cos + swapped * sin_signed


def _in_proj(x, g_mix, w_in, b_in, cos64, sin64, cos128, sin128):
    h = _rms(x, g_mix).astype(BF16)
    z = _dot(h, w_in) + b_in

    def cols(off, width):
        return [z[:, off + c * LANES: off + (c + 1) * LANES] for c in range(width // LANES)]

    scale_q = ATT_HEAD_DIM ** -0.5
    aq = [_rope(c, cos64, sin64, ATT_HEAD_DIM // 2) * scale_q for c in cols(O_AQ, ATT_Q_W)]
    ak = _rope(z[:, O_AK:O_AK + LANES], cos64, sin64, ATT_HEAD_DIM // 2)
    av = z[:, O_AV:O_AV + LANES]
    rq = [_rope(c, cos128, sin128, RET_DK // 2) for c in cols(O_RQ, RET_W)]
    rk = [_rope(c, cos128, sin128, RET_DK // 2) * (RET_DK ** -0.5) for c in cols(O_RK, RET_W)]
    rv = cols(O_RV, RET_W)
    rg = z[:, O_RG:O_RG + RET_W]
    return aq, ak, av, rq, rk, rv, rg


def _expand_q(aq_cols, rows):
    lane = lax.broadcasted_iota(jnp.int32, (rows, LANES), 1)
    low = lane < ATT_HEAD_DIM
    pieces = []
    for h in range(ATT_HEADS):
        col, half, kv = aq_cols[h // 2], h % 2, h // ATT_GROUP
        if half != kv:
            col = pltpu.roll(col, ATT_HEAD_DIM, 1)
        pieces.append(jnp.where(low if kv == 0 else jnp.logical_not(low), col, 0.0))
    return jnp.concatenate(pieces, axis=0)


def _collect_heads(o, rows):
    lane = lax.broadcasted_iota(jnp.int32, (rows, LANES), 1)
    low = lane < ATT_HEAD_DIM
    out = []
    for c in range(ATT_HEADS // 2):
        pieces = []
        for half in (0, 1):
            h = 2 * c + half
            oh = o[h * rows:(h + 1) * rows]
            if h // ATT_GROUP != half:
                oh = pltpu.roll(oh, ATT_HEAD_DIM, 1)
            pieces.append(oh)
        out.append(jnp.where(low, pieces[0], pieces[1]))
    return jnp.concatenate(out, axis=1)


def _sink_softmax(s, mask, sink):
    s = jnp.where(mask, s, NEG_INF)
    m = jnp.maximum(jnp.max(s, axis=-1, keepdims=True), sink)
    e = jnp.exp(s - m)
    den = jnp.sum(e, axis=-1, keepdims=True) + jnp.exp(sink - m)
    return e * (1.0 / den)


def _group_norm_gate(ret, rg, g_ret):
    cols = []
    for h in range(RET_HEADS):
        of = ret[:, h * RET_DV:(h + 1) * RET_DV]
        d = of - jnp.mean(of, axis=-1, keepdims=True)
        var = jnp.mean(d * d, axis=-1, keepdims=True)
        cols.append(d * lax.rsqrt(var + GN_EPS))
    y = jnp.concatenate(cols, axis=1) * g_ret
    return y * (rg * jax.nn.sigmoid(rg))


def _route(x1, g_mlp, wr_hi, wr_lo, b_router, cnt):
    n = x1.shape[0]
    h2 = _rms(x1, g_mlp)
    hi = h2.astype(BF16)
    lo = (h2 - hi.astype(F32)).astype(BF16)
    logits = _dot(hi, wr_hi) + (_dot(hi, wr_lo) + _dot(lo, wr_hi)) + b_router
    work = logits.T[:N_EXPERTS]
    expert = lax.broadcasted_iota(jnp.int32, (N_EXPERTS, n), 0)
    idxs, vals = [], []
    for _ in range(TOP_K):
        m = jnp.max(work, axis=0, keepdims=True)
        idx = jnp.min(jnp.where(work == m, expert, N_EXPERTS), axis=0, keepdims=True)
        idxs.append(idx)
        vals.append(m)
        work = jnp.where(expert == idx, -jnp.inf, work)
    es = [jnp.exp(v - vals[0]) for v in vals]
    den = es[0] + es[1] + es[2] + es[3]
    gates = [e / den for e in es]

    sel = (expert == idxs[0]) | (expert == idxs[1]) | (expert == idxs[2]) | (expert == idxs[3])
    onehot = jnp.where(sel, 1.0, 0.0)
    r = lax.broadcasted_iota(jnp.int32, (n, n), 0)
    c = lax.broadcasted_iota(jnp.int32, (n, n), 1)
    earlier = jnp.where(r < c, 1.0, 0.0).astype(BF16)
    before = _dot(onehot.astype(BF16), earlier) + cnt
    row = lax.broadcasted_iota(jnp.int32, (LANES, n), 0)
    meta_t = jnp.zeros((LANES, n), F32)
    for k in range(TOP_K):
        rank = jnp.sum(jnp.where(expert == idxs[k], before, 0.0), axis=0, keepdims=True)
        meta_t = jnp.where(row == META_IDX + k, idxs[k].astype(F32), meta_t)
        meta_t = jnp.where(row == META_RANK + k, rank, meta_t)
        meta_t = jnp.where(row == META_GATE + k, gates[k], meta_t)
    return h2, meta_t.T, cnt + jnp.sum(onehot, axis=1, keepdims=True)


def _mixer_prompt_kernel(scal_ref, x_ref, cos64_ref, sin64_ref, cos128_ref, sin128_ref,
                         intra_ref, cross_ref, kdec_ref,
                         g_mix_ref, w_in_ref, b_in_ref, g_ret_ref, w_out_ref, b_out_ref,
                         g_mlp_ref, wr_hi_ref, wr_lo_ref, b_router_ref,
                         x1_ref, h2_ref, meta_ref, cnt_out_ref, kp_ref, vp_ref, stp_ref,
                         prevk, prevv, st, cnt):
    b, n = pl.program_id(0), pl.program_id(1)

    @pl.when((b == 0) & (n == 0))
    def _():
        cnt[...] = jnp.zeros_like(cnt)

    @pl.when(n == 0)
    def _():
        prevk[...] = jnp.zeros_like(prevk)
        prevv[...] = jnp.zeros_like(prevv)
        st[...] = jnp.zeros_like(st)

    x = x_ref[...]
    aq, ak, av, rq, rk, rv, rg = _in_proj(
        x, g_mix_ref[...], w_in_ref[...], b_in_ref[...],
        cos64_ref[...], sin64_ref[...], cos128_ref[...], sin128_ref[...])

    row = lax.broadcasted_iota(jnp.int32, (BLK, 2 * BLK), 0)
    col = lax.broadcasted_iota(jnp.int32, (BLK, 2 * BLK), 1)
    mask_prev = (col < BLK) & (col > row)
    mask_cur = (col >= BLK) & (col - BLK <= row)

    k_prev, v_prev = prevk[...], prevv[...]
    mixed = []
    for j in range(MIX_TOKENS // BLK):
        sl = slice(j * BLK, (j + 1) * BLK)
        kk = jnp.concatenate([k_prev, ak[sl]], axis=0).astype(BF16)
        vv = jnp.concatenate([v_prev, av[sl]], axis=0).astype(BF16)
        qall = _expand_q([cq[sl] for cq in aq], BLK).astype(BF16)
        s = _dot_nt(qall, kk)
        mask = mask_cur | mask_prev if j > 0 else mask_cur | (mask_prev & (n > 0))
        probs = [_sink_softmax(s[h * BLK:(h + 1) * BLK], mask, scal_ref[h]) for h in range(ATT_HEADS)]
        o = _dot(jnp.concatenate(probs, axis=0).astype(BF16), vv)
        att = _collect_heads(o, BLK)
        k_prev, v_prev = ak[sl], av[sl]

        ret = []
        for h in range(RET_HEADS):
            qb, kh, vb = rq[h][sl].astype(BF16), rk[h][sl], rv[h][sl].astype(BF16)
            sc = _dot_nt(qb, kh.astype(BF16)) * intra_ref[h]
            sth = st[h]
            o_h = _dot(sc.astype(BF16), vb) + _dot(qb, sth.astype(BF16)) * cross_ref[h]
            st[h] = sth * scal_ref[ATT_HEADS + h] + _dot_tn((kh * kdec_ref[h]).astype(BF16), vb)
            ret.append(o_h)
        ret_out = _group_norm_gate(jnp.concatenate(ret, axis=1), rg[sl], g_ret_ref[...])
        mixed.append(jnp.concatenate([att, ret_out], axis=1))

    mixed = jnp.concatenate(mixed, axis=0).astype(BF16)
    x1 = x + _dot(mixed, w_out_ref[...]) + b_out_ref[...]
    x1_ref[...] = x1

    c = cnt[...]
    for j in range(MIX_TOKENS // BLK):
        sl = slice(j * BLK, (j + 1) * BLK)
        h2, meta, c = _route(x1[sl], g_mlp_ref[...], wr_hi_ref[...], wr_lo_ref[...], b_router_ref[...], c)
        _store_token_tiles(h2_ref, j * BLK, h2)
        meta_ref[sl, :] = meta
    cnt[...] = c
    cnt_out_ref[...] = c

    prevk[...] = k_prev
    prevv[...] = v_prev
    kp_ref[0] = k_prev
    vp_ref[0] = v_prev
    stp_ref[0] = st[...]


def _const_spec(shape):
    nd = len(shape)
    return pl.BlockSpec(shape, lambda *_: (0,) * nd)


def _mixer_prompt(scal, x, tabs, ret_tabs, wts):
    bsz, seq, _ = x.shape
    steps = seq // MIX_TOKENS
    x2 = x.reshape(bsz * seq, D_MODEL)
    tok = lambda w: pl.BlockSpec((MIX_TOKENS, w), lambda b, n: (b * steps + n, 0))
    pos = lambda w: pl.BlockSpec((MIX_TOKENS, w), lambda b, n: (n, 0))
    in_specs = ([pl.BlockSpec(memory_space=pltpu.SMEM), tok(D_MODEL)]
                + [pos(LANES)] * 4
                + [_const_spec(t.shape) for t in ret_tabs]
                + [_const_spec(w.shape) for w in wts])
    out_shape = (
        jax.ShapeDtypeStruct((bsz * seq, D_MODEL), F32),
        jax.ShapeDtypeStruct(_token_shape(bsz * seq), F32),
        jax.ShapeDtypeStruct((bsz * seq, LANES), F32),
        jax.ShapeDtypeStruct((N_EXPERTS, LANES), F32),
        jax.ShapeDtypeStruct((bsz, BLK, LANES), F32),
        jax.ShapeDtypeStruct((bsz, BLK, LANES), F32),
        jax.ShapeDtypeStruct((bsz, RET_HEADS, RET_DK, RET_DV), F32),
    )
    out_specs = (
        tok(D_MODEL), pl.BlockSpec(_token_shape(MIX_TOKENS), lambda b, n: (0, b * steps + n, 0)),
        tok(LANES), _const_spec((N_EXPERTS, LANES)),
        pl.BlockSpec((1, BLK, LANES), lambda b, n: (b, 0, 0)),
        pl.BlockSpec((1, BLK, LANES), lambda b, n: (b, 0, 0)),
        pl.BlockSpec((1, RET_HEADS, RET_DK, RET_DV), lambda b, n: (b, 0, 0, 0)),
    )
    return pl.pallas_call(
        _mixer_prompt_kernel,
        grid=(bsz, steps),
        in_specs=in_specs,
        out_specs=out_specs,
        out_shape=out_shape,
        scratch_shapes=[pltpu.VMEM((BLK, LANES), F32), pltpu.VMEM((BLK, LANES), F32),
                        pltpu.VMEM((RET_HEADS, RET_DK, RET_DV), F32), pltpu.VMEM((N_EXPERTS, LANES), F32)],
        compiler_params=pltpu.CompilerParams(
            dimension_semantics=("arbitrary", "arbitrary"), vmem_limit_bytes=VMEM_LIMIT),
        name="mixer_prompt",
    )(scal, x2, *tabs, *ret_tabs, *wts)


def _sample_inproj_kernel(x_ref, cos64_ref, sin64_ref, cos128_ref, sin128_ref,
                          g_mix_ref, w_in_ref, b_in_ref,
                          aq_ref, ak_ref, av_ref, rq_ref, rk_ref, rv_ref, rg_ref):
    aq, ak, av, rq, rk, rv, rg = _in_proj(
        x_ref[...], g_mix_ref[...], w_in_ref[...], b_in_ref[...],
        cos64_ref[...], sin64_ref[...], cos128_ref[...], sin128_ref[...])
    aq_ref[...] = jnp.concatenate(aq, axis=1)
    ak_ref[...] = ak
    av_ref[...] = av
    rq_ref[...] = jnp.concatenate(rq, axis=1)
    rk_ref[...] = jnp.concatenate(rk, axis=1)
    rv_ref[...] = jnp.concatenate(rv, axis=1)
    rg_ref[...] = rg


def _sample_inproj(x, tabs, g_mix, w_in, b_in):
    n = x.shape[0]
    widths = (ATT_Q_W, LANES, LANES, RET_W, RET_W, RET_W, RET_W)
    args = (x, *tabs, g_mix, w_in, b_in)
    return pl.pallas_call(
        _sample_inproj_kernel,
        grid=(1,),
        in_specs=[_const_spec(a.shape) for a in args],
        out_specs=tuple(_const_spec((n, w)) for w in widths),
        out_shape=tuple(jax.ShapeDtypeStruct((n, w), F32) for w in widths),
        compiler_params=pltpu.CompilerParams(vmem_limit_bytes=VMEM_LIMIT),
        name="sample_inproj",
    )(*args)


def _sample_state_kernel(scal_ref, aq_ref, ak_ref, av_ref, rq_ref, rk_ref, rv_ref,
                         kbuf_ref, vbuf_ref, st_ref,
                         att_ref, ret_ref, kout_ref, vout_ref, stout_ref):
    nb = SAMPLE_CHUNK
    rows = nb * WINDOW

    def shift_in(buf_ref, new):
        flat = buf_ref[...].reshape(rows, LANES)
        rolled = pltpu.roll(flat, rows - 1, 0)
        rolled = rolled.reshape(nb, WINDOW, LANES)
        pos = lax.broadcasted_iota(jnp.int32, (nb, WINDOW, LANES), 1)
        return jnp.where(pos == WINDOW - 1, new[:, None, :], rolled)

    knew = shift_in(kbuf_ref, ak_ref[...])
    vnew = shift_in(vbuf_ref, av_ref[...])
    kout_ref[...] = knew
    vout_ref[...] = vnew

    aq = aq_ref[...]
    qall = _expand_q([aq[:, c * LANES:(c + 1) * LANES] for c in range(ATT_Q_W // LANES)], nb)
    s = _dot_nt(qall.astype(BF16), knew.reshape(rows, LANES).astype(BF16))
    r = lax.broadcasted_iota(jnp.int32, (nb, rows), 0)
    c = lax.broadcasted_iota(jnp.int32, (nb, rows), 1)
    own = (c // WINDOW) == r
    probs = [_sink_softmax(s[h * nb:(h + 1) * nb], own, scal_ref[h]) for h in range(ATT_HEADS)]
    o = _dot(jnp.concatenate(probs, axis=0).astype(BF16), vnew.reshape(rows, LANES).astype(BF16))
    att_ref[...] = _collect_heads(o, nb)

    qt = rq_ref[...].T
    kt = rk_ref[...].T
    rv = rv_ref[...]
    for bi in range(nb):
        for h in range(RET_HEADS):
            hs = slice(h * RET_DK, (h + 1) * RET_DK)
            s_new = st_ref[bi, h] * scal_ref[ATT_HEADS + h] + kt[hs, bi:bi + 1] * rv[bi:bi + 1, hs]
            stout_ref[bi, h] = s_new
            ret_ref[bi:bi + 1, hs] = jnp.sum(qt[hs, bi:bi + 1] * s_new, axis=0, keepdims=True)


def _sample_state(scal, aq, ak, av, rq, rk, rv, kbuf, vbuf, st):
    n = aq.shape[0]
    nb = SAMPLE_CHUNK
    row = lambda w: pl.BlockSpec((nb, w), lambda i: (i, 0))
    buf = pl.BlockSpec((nb, WINDOW, LANES), lambda i: (i, 0, 0))
    stt = pl.BlockSpec((nb, RET_HEADS, RET_DK, RET_DV), lambda i: (i, 0, 0, 0))
    return pl.pallas_call(
        _sample_state_kernel,
        grid=(n // nb,),
        in_specs=[pl.BlockSpec(memory_space=pltpu.SMEM), row(ATT_Q_W), row(LANES), row(LANES),
                  row(RET_W), row(RET_W), row(RET_W), buf, buf, stt],
        out_specs=(row(ATT_Q_W), row(RET_W), buf, buf, stt),
        out_shape=(jax.ShapeDtypeStruct((n, ATT_Q_W), F32), jax.ShapeDtypeStruct((n, RET_W), F32),
                   jax.ShapeDtypeStruct(kbuf.shape, F32), jax.ShapeDtypeStruct(vbuf.shape, F32),
                   jax.ShapeDtypeStruct(st.shape, F32)),
        compiler_params=pltpu.CompilerParams(
            dimension_semantics=("arbitrary",), vmem_limit_bytes=VMEM_LIMIT),
        name="sample_state",
    )(scal, aq, ak, av, rq, rk, rv, kbuf, vbuf, st)


def _sample_post_kernel(x_ref, att_ref, ret_ref, rg_ref, cnt_in_ref,
                        g_ret_ref, w_out_ref, b_out_ref, g_mlp_ref, wr_hi_ref, wr_lo_ref, b_router_ref,
                        x1_ref, h2_ref, meta_ref, cnt_out_ref):
    x = x_ref[...]
    ret_out = _group_norm_gate(ret_ref[...], rg_ref[...], g_ret_ref[...])
    mixed = jnp.concatenate([att_ref[...], ret_out], axis=1).astype(BF16)
    x1 = x + _dot(mixed, w_out_ref[...]) + b_out_ref[...]
    x1_ref[...] = x1
    h2, meta, c = _route(x1, g_mlp_ref[...], wr_hi_ref[...], wr_lo_ref[...], b_router_ref[...],
                         cnt_in_ref[...])
    _store_token_tiles(h2_ref, 0, h2)
    meta_ref[...] = meta
    cnt_out_ref[...] = c


def _sample_post(x, att, ret, rg, cnt, wts):
    n = x.shape[0]
    vm = [x, att, ret, rg, cnt, *wts]
    shapes = ((n, D_MODEL), _token_shape(n), (n, LANES), (N_EXPERTS, LANES))
    return pl.pallas_call(
        _sample_post_kernel,
        grid=(1,),
        in_specs=[_const_spec(a.shape) for a in vm],
        out_specs=tuple(_const_spec(s) for s in shapes),
        out_shape=tuple(jax.ShapeDtypeStruct(s, F32) for s in shapes),
        compiler_params=pltpu.CompilerParams(vmem_limit_bytes=VMEM_LIMIT),
        name="sample_post",
    )(*vm)


def _row_copy(src, src_token, dst, dst_token, sem):
    rows = lambda tok: pl.ds(pl.multiple_of(tok * TOKEN_ROWS, TOKEN_ROWS), TOKEN_ROWS)
    return pltpu.make_async_copy(src.at[:, rows(src_token)], dst.at[:, rows(dst_token)], sem)


def _for_each_row(n_tok, fn):
    def body(g, carry):
        for u in range(ROW_UNROLL):
            for k in range(TOP_K):
                fn(g * ROW_UNROLL + u, k)
        return carry

    lax.fori_loop(0, n_tok // ROW_UNROLL, body, 0)


def _dispatch_kernel(prompt_steps, n_dec, pos_p, pos_s, h2p_ref, h2s_ref, xs_hbm, sem_rows):
    i = pl.program_id(0)

    def scatter_rows(src_ref, n_tok, pos_ref, first_idx):
        def copy(t, k):
            return _row_copy(src_ref, t, xs_hbm, pos_ref[first_idx + t * TOP_K + k], sem_rows)

        _for_each_row(n_tok, lambda t, k: copy(t, k).start(priority=k % 2))
        _for_each_row(n_tok, lambda t, k: _row_copy(src_ref, 0, xs_hbm, 0, sem_rows).wait())

    @pl.when(i < prompt_steps)
    def _():
        scatter_rows(h2p_ref, DISPATCH_TOKENS, pos_p, i * (DISPATCH_TOKENS * TOP_K))

    @pl.when(i == prompt_steps)
    def _():
        scatter_rows(h2s_ref, n_dec, pos_s, 0)


def _dispatch(pos_p, pos_s, h2_p, h2_s):
    n_prompt, n_dec = h2_p.shape[1] // TOKEN_ROWS, h2_s.shape[1] // TOKEN_ROWS
    prompt_steps = n_prompt // DISPATCH_TOKENS
    grid_spec = pltpu.PrefetchScalarGridSpec(
        num_scalar_prefetch=2,
        grid=(prompt_steps + 1,),
        in_specs=[pl.BlockSpec(_token_shape(DISPATCH_TOKENS),
                               lambda i, *_: (0, jnp.minimum(i, prompt_steps - 1), 0)),
                  pl.BlockSpec(h2_s.shape, lambda i, *_: (0, 0, 0))],
        out_specs=pl.BlockSpec(memory_space=pl.ANY),
        scratch_shapes=[pltpu.SemaphoreType.DMA],
    )
    return pl.pallas_call(
        functools.partial(_dispatch_kernel, prompt_steps, n_dec),
        grid_spec=grid_spec,
        out_shape=jax.ShapeDtypeStruct(_token_shape((n_prompt + n_dec) * TOP_K), F32),
        compiler_params=pltpu.CompilerParams(
            dimension_semantics=("arbitrary",), vmem_limit_bytes=VMEM_LIMIT),
        name="dispatch",
    )(pos_p, pos_s, h2_p, h2_s)


def _experts_kernel(v_tile, v_expert, v_lo, v_hi, v_newtile, v_newexp, v_next, n_visits,
                    xs_ref, w_in_hbm, b_in_ref, w_out_hbm, b_out_ref, y_ref,
                    w_in_f32, w_out_f32, w_in_bf, w_out_bf, sems):
    v = pl.program_id(0)

    def weight_copies(e):
        return (pltpu.make_async_copy(w_in_hbm.at[e], w_in_f32, sems.at[0]),
                pltpu.make_async_copy(w_out_hbm.at[e], w_out_f32, sems.at[1]))

    @pl.when(v < n_visits[0])
    def _():
        @pl.when(v_newexp[v] == 1)
        def _():
            @pl.when(v == 0)
            def _():
                for cp in weight_copies(v_expert[0]):
                    cp.start(priority=1)

            for cp in weight_copies(v_expert[v]):
                cp.wait()
            w_in_bf[...] = w_in_f32[...].astype(BF16)
            w_out_bf[...] = w_out_f32[...].astype(BF16)

            @pl.when(v_next[v] >= 0)
            def _():
                for cp in weight_copies(v_next[v]):
                    cp.start(priority=1)

        x = _load_token_tiles(xs_ref, EXPERT_ROWS).astype(BF16)
        hg = _dot(x, w_in_bf[...]) + b_in_ref[0]
        gate = jnp.minimum(hg[:, :D_FF], SWIGLU_LIMIT)
        lin = jnp.clip(hg[:, D_FF:], -SWIGLU_LIMIT, SWIGLU_LIMIT)
        act = gate * jax.nn.sigmoid(SWIGLU_ALPHA * gate) * (lin + 1.0)
        y = _dot(act.astype(BF16), w_out_bf[...]) + b_out_ref[0]

        row = lax.broadcasted_iota(jnp.int32, (EXPERT_ROWS, LANES), 0)
        mine = (row >= v_lo[v]) & (row < v_hi[v])

        def write_rows(first_visit_of_tile):
            for j, c, cols in _token_chunks():
                rows_c = pl.ds(c, EXPERT_ROWS, stride=TOKEN_ROWS)
                keep = 0.0 if first_visit_of_tile else y_ref[j, rows_c, :]
                y_ref[j, rows_c, :] = jnp.where(mine, y[:, cols], keep)

        @pl.when(v_newtile[v] == 1)
        def _():
            write_rows(True)

        @pl.when(v_newtile[v] == 0)
        def _():
            write_rows(False)


def _experts(visits, xs, w_e_in, b_e_in, w_e_out, b_e_out):
    n_rows = xs.shape[1] // TOKEN_ROWS
    max_visits = n_rows // EXPERT_ROWS + N_EXPERTS - 1
    tile_rows = pl.BlockSpec(_token_shape(EXPERT_ROWS), lambda v, vt, *_: (0, vt[v], 0))
    any_spec = pl.BlockSpec(memory_space=pl.ANY)
    grid_spec = pltpu.PrefetchScalarGridSpec(
        num_scalar_prefetch=len(visits),
        grid=(max_visits,),
        in_specs=[
            tile_rows, any_spec,
            pl.BlockSpec((1, 1, 2 * D_FF), lambda v, vt, ve, *_: (ve[v], 0, 0)),
            any_spec,
            pl.BlockSpec((1, 1, D_MODEL), lambda v, vt, ve, *_: (ve[v], 0, 0)),
        ],
        out_specs=tile_rows,
        scratch_shapes=[pltpu.VMEM((D_MODEL, 2 * D_FF), F32), pltpu.VMEM((D_FF, D_MODEL), F32),
                        pltpu.VMEM((D_MODEL, 2 * D_FF), BF16), pltpu.VMEM((D_FF, D_MODEL), BF16),
                        pltpu.SemaphoreType.DMA((2,))],
    )
    return pl.pallas_call(
        _experts_kernel,
        grid_spec=grid_spec,
        out_shape=jax.ShapeDtypeStruct(xs.shape, F32),
        compiler_params=pltpu.CompilerParams(
            dimension_semantics=("arbitrary",), vmem_limit_bytes=VMEM_LIMIT),
        name="experts",
    )(*visits, xs, w_e_in, b_e_in.reshape(N_EXPERTS, 1, 2 * D_FF), w_e_out,
      b_e_out.reshape(N_EXPERTS, 1, D_MODEL))


def _visit_schedule(counts, n_rows):
    n_tiles = n_rows // EXPERT_ROWS
    max_visits = n_tiles + N_EXPERTS - 1
    ends = jnp.cumsum(counts)
    starts = ends - counts
    first_tile = starts // EXPERT_ROWS
    tiles_e = jnp.where(counts > 0, (ends - 1) // EXPERT_ROWS - first_tile + 1, 0)
    vis_end = jnp.cumsum(tiles_e)
    vis_start = vis_end - tiles_e
    total = vis_end[-1]
    v = jnp.minimum(jnp.arange(max_visits, dtype=jnp.int32), total - 1)
    e = jnp.sum((vis_end[None, :] <= v[:, None]).astype(jnp.int32), axis=1)
    is_e = e[:, None] == jnp.arange(N_EXPERTS, dtype=jnp.int32)[None, :]
    pick = lambda table: jnp.sum(jnp.where(is_e, table[None, :], 0), axis=1)
    tile = pick(first_tile) + (v - pick(vis_start))
    lo = jnp.maximum(pick(starts), tile * EXPERT_ROWS) - tile * EXPERT_ROWS
    hi = jnp.minimum(pick(ends), (tile + 1) * EXPERT_ROWS) - tile * EXPERT_ROWS
    prev_tile = jnp.concatenate([jnp.full((1,), -1, jnp.int32), tile[:-1]])
    prev_e = jnp.concatenate([jnp.full((1,), -1, jnp.int32), e[:-1]])
    experts = jnp.arange(N_EXPERTS, dtype=jnp.int32)
    later = (experts[None, :] > experts[:, None]) & (tiles_e[None, :] > 0)
    next_e = jnp.min(jnp.where(later, experts[None, :], N_EXPERTS), axis=1)
    next_e = jnp.where(next_e == N_EXPERTS, -1, next_e)
    as_i32 = lambda a: a.astype(jnp.int32)
    return (as_i32(tile), as_i32(e), as_i32(lo), as_i32(hi), as_i32(tile != prev_tile),
            as_i32(e != prev_e), as_i32(pick(next_e)), as_i32(total).reshape(1))


def _combine_kernel(tokens, pos_ref, x1_ref, meta_ref, p_ref, ys_hbm,
                    g_ple_ref, w_pg_ref, w_ple_ref, g_final_ref, y_ref, rows_even, rows_odd, sem_rows):
    i, n = pl.program_id(0), pl.num_programs(0)

    def copy(block, t, k, rows, sem):
        src_row = pos_ref[(block * tokens + t) * TOP_K + k]
        return _row_copy(ys_hbm, src_row, rows.at[k], t, sem)

    def wait_block(rows, sem):
        _for_each_row(tokens, lambda t, k: _row_copy(ys_hbm, 0, rows.at[0], 0, sem).wait())

    @pl.when(i == 0)
    def _():
        _for_each_row(tokens, lambda t, k: copy(0, t, k, rows_even, sem_rows.at[0]).start(priority=k % 2))

    def step(rows, sem, next_rows, next_sem):
        wait_block(rows, sem)
        next_block = lax.rem(i + 1, n)
        for t in range(tokens):
            for k in range(TOP_K):
                copy(next_block, t, k, next_rows, next_sem).start(priority=k % 2)

        meta = meta_ref[...]
        x1 = x1_ref[...]
        gates = [jnp.broadcast_to(meta[:, META_GATE + k:META_GATE + k + 1], (tokens, LANES))
                 for k in range(TOP_K)]
        cols = []
        for j, c, xcols in _token_chunks():
            acc = x1[:, xcols]
            for k in range(TOP_K):
                acc = acc + gates[k] * rows[k, j, pl.ds(c, tokens, stride=TOKEN_ROWS), :]
            cols.append(acc)
        x2 = jnp.concatenate(cols, axis=1)
        hp = _rms(x2, g_ple_ref[...]).astype(BF16)
        gate = jax.nn.sigmoid(_dot(hp, w_pg_ref[...]))
        x3 = x2 + _dot(p_ref[...].astype(BF16), w_ple_ref[...]) * gate
        y_ref[...] = _rms(x3, g_final_ref[...])

        @pl.when(i == n - 1)
        def _():
            wait_block(next_rows, next_sem)

    @pl.when(i % 2 == 0)
    def _():
        step(rows_even, sem_rows.at[0], rows_odd, sem_rows.at[1])

    @pl.when(i % 2 == 1)
    def _():
        step(rows_odd, sem_rows.at[1], rows_even, sem_rows.at[0])


def _combine(tokens, pos, x1, meta, p, ys, wts):
    n_out = x1.shape[0]
    tok = lambda w: pl.BlockSpec((tokens, w), lambda i, *_: (i, 0))
    grid_spec = pltpu.PrefetchScalarGridSpec(
        num_scalar_prefetch=1,
        grid=(n_out // tokens,),
        in_specs=[tok(D_MODEL), tok(LANES), tok(PLE_DIM), pl.BlockSpec(memory_space=pl.ANY)]
                 + [pl.BlockSpec(w.shape, lambda i, *_: (0, 0)) for w in wts],
        out_specs=tok(D_MODEL),
        scratch_shapes=[pltpu.VMEM((TOP_K,) + _token_shape(tokens), F32),
                        pltpu.VMEM((TOP_K,) + _token_shape(tokens), F32),
                        pltpu.SemaphoreType.DMA((2,))],
    )
    return pl.pallas_call(
        functools.partial(_combine_kernel, tokens),
        grid_spec=grid_spec,
        out_shape=jax.ShapeDtypeStruct((n_out, D_MODEL), F32),
        compiler_params=pltpu.CompilerParams(
            dimension_semantics=("arbitrary",), vmem_limit_bytes=VMEM_LIMIT),
        name="combine_ple",
    )(pos, x1, meta, p, ys, *wts)


def _rope_tables(pos, n_rows):
    out = []
    for dim in (ATT_HEAD_DIM, RET_DK):
        half = dim // 2
        inv = ROPE_THETA ** (-np.arange(half, dtype=np.float64) / half)
        ang = np.asarray(pos, np.float64)[:, None] * inv[None, :]
        cos, sin = np.cos(ang), np.sin(ang)
        reps = LANES // dim
        cos_row = np.tile(np.concatenate([cos, cos], axis=-1), (1, reps))
        sin_row = np.tile(np.concatenate([-sin, sin], axis=-1), (1, reps))
        out += [jnp.asarray(np.broadcast_to(cos_row, (n_rows, LANES)), F32),
                jnp.asarray(np.broadcast_to(sin_row, (n_rows, LANES)), F32)]
    return out


def _retention_tables():
    log_g = np.log(1.0 - 2.0 ** (-5.0 - np.arange(RET_HEADS, dtype=np.float64)))
    t = np.arange(BLK, dtype=np.float64)
    rel = t[:, None] - t[None, :]
    intra = np.where(rel >= 0, np.exp(np.maximum(rel, 0.0)[None] * log_g[:, None, None]), 0.0)
    cross = np.exp((t + 1.0)[None, :] * log_g[:, None])
    kdec = np.exp((BLK - 1.0 - t)[None, :] * log_g[:, None])
    lane_bcast = lambda a: np.broadcast_to(a[:, :, None], (RET_HEADS, BLK, LANES))
    tabs = tuple(jnp.asarray(a, F32) for a in (intra, lane_bcast(cross), lane_bcast(kdec)))
    return log_g, tabs


def kernel(x_prompt, x_sample, state_swa_k, state_swa_v, state_ret, p_prompt, p_sample, g_mix, w_in, b_in,
           attn_sinks, g_ret_norm, w_out, b_out, g_mlp, w_router, b_router, w_e_in, b_e_in, w_e_out,
           b_e_out, g_ple, w_ple_gate, w_ple, g_final):
    bsz, seq, _ = x_prompt.shape
    n_dec = x_sample.shape[0]
    n_prompt = bsz * seq
    assert x_sample.shape[1] == 1 and g_mix.shape[0] == 1
    assert state_swa_k.shape[2] == WINDOW and n_dec <= DISPATCH_TOKENS

    row = lambda a: a.reshape(1, -1)
    log_g, ret_tabs = _retention_tables()
    sinks = attn_sinks[0].astype(F32)
    scal_prompt = jnp.concatenate([sinks, jnp.asarray(np.exp(BLK * log_g), F32)])
    scal_sample = jnp.concatenate([sinks, jnp.asarray(np.exp(log_g), F32)])

    w_in_bf = w_in[0].astype(BF16)
    w_out_bf = w_out[0].astype(BF16)
    wr = jnp.pad(w_router[0], ((0, 0), (0, LANES - N_EXPERTS)))
    wr_hi = wr.astype(BF16)
    wr_lo = (wr - wr_hi.astype(F32)).astype(BF16)
    br = jnp.pad(b_router[0], (0, LANES - N_EXPERTS)).reshape(1, LANES)
    post_wts = (row(g_ret_norm[0]), w_out_bf, row(b_out[0]), row(g_mlp[0]), wr_hi, wr_lo, br)

    tabs_p = _rope_tables(np.arange(seq), seq)
    x1_p, h2_p, meta_p, cnt, kp, vp, st_p = _mixer_prompt(
        scal_prompt, x_prompt, tabs_p, ret_tabs, (row(g_mix[0]), w_in_bf, row(b_in[0])) + post_wts)

    tabs_s = _rope_tables(np.full((1,), PAST_LEN), n_dec)
    xs2 = x_sample.reshape(n_dec, D_MODEL)
    aq, ak, av, rq, rk, rv, rg = _sample_inproj(xs2, tabs_s, row(g_mix[0]), w_in_bf, row(b_in[0]))
    att, ret, ks, vs, st_s = _sample_state(
        scal_sample, aq, ak, av, rq, rk, rv,
        state_swa_k[0].reshape(n_dec, WINDOW, LANES), state_swa_v[0].reshape(n_dec, WINDOW, LANES),
        state_ret[0])
    x1_s, h2_s, meta_s, cnt = _sample_post(xs2, att, ret, rg, cnt, post_wts)

    counts = cnt[:, 0].astype(jnp.int32)
    starts = jnp.cumsum(counts) - counts
    def sorted_rows(meta):
        sel = meta[:, META_IDX:META_IDX + TOP_K].astype(jnp.int32)
        rank = meta[:, META_RANK:META_RANK + TOP_K].astype(jnp.int32)
        is_e = sel[:, :, None] == jnp.arange(N_EXPERTS, dtype=jnp.int32)[None, None, :]
        return (jnp.sum(jnp.where(is_e, starts[None, None, :], 0), axis=-1) + rank).reshape(-1)

    pos_p, pos_s = sorted_rows(meta_p), sorted_rows(meta_s)

    xs_sorted = _dispatch(pos_p, pos_s, h2_p, h2_s)
    visits = _visit_schedule(counts, (n_prompt + n_dec) * TOP_K)
    ys_sorted = _experts(visits, xs_sorted, w_e_in[0], b_e_in[0], w_e_out[0], b_e_out[0])

    ple_wts = (row(g_ple[0]), w_ple_gate[0].astype(BF16), w_ple[0].astype(BF16), row(g_final))
    y_p = _combine(COMBINE_TOKENS, pos_p, x1_p, meta_p, p_prompt[0].reshape(n_prompt, PLE_DIM),
                   ys_sorted, ple_wts)
    y_s = _combine(n_dec, pos_s, x1_s, meta_s, p_sample[0].reshape(n_dec, PLE_DIM), ys_sorted, ple_wts)

    kv_shape = (1, bsz, WINDOW, ATT_KV_HEADS, ATT_HEAD_DIM)
    dec_shape = (1, n_dec, WINDOW, ATT_KV_HEADS, ATT_HEAD_DIM)
    return (y_p.reshape(bsz, seq, D_MODEL), y_s.reshape(n_dec, 1, D_MODEL),
            kp.reshape(kv_shape), vp.reshape(kv_shape), st_p[None],
            ks.reshape(dec_shape), vs.reshape(dec_shape), st_s[None])
```

```python
import functools

import jax
import jax.numpy as jnp
import numpy as np
from jax import lax
from jax.experimental import pallas as pl
from jax.experimental.pallas import tpu as pltpu

D_MODEL = 1024
PAST_LEN = 16384
PLE_DIM = 256
ATT_HEADS = 8
ATT_KV_HEADS = 2
ATT_HEAD_DIM = 64
ATT_GROUP = ATT_HEADS // ATT_KV_HEADS
WINDOW = 128
BLK = 128
RET_HEADS = 4
RET_DK = 128
RET_DV = 128
ROPE_THETA = 10000.0
N_EXPERTS = 32
TOP_K = 4
D_FF = 1024
SWIGLU_LIMIT = 7.0
SWIGLU_ALPHA = 1.702
NORM_EPS = 1e-5
GN_EPS = 1e-6
NEG_INF = -1e30

ATT_Q_W = ATT_HEADS * ATT_HEAD_DIM
ATT_KV_W = ATT_KV_HEADS * ATT_HEAD_DIM
RET_W = RET_HEADS * RET_DK
IN_W = ATT_Q_W + 2 * ATT_KV_W + 4 * RET_W
O_AQ = 0
O_AK = ATT_Q_W
O_AV = O_AK + ATT_KV_W
O_RQ = O_AV + ATT_KV_W
O_RK = O_RQ + RET_W
O_RV = O_RK + RET_W
O_RG = O_RV + RET_W

LANES = 128
META_IDX, META_RANK, META_GATE = 0, 4, 8

MIX_TOKENS = 512
EXPERT_ROWS = 256
COMBINE_TOKENS = 256
SAMPLE_CHUNK = 16
ROW_UNROLL = 16
TABLE_UNROLL = 8
TABLE_SRC_BITS = 15
SPARE_FILL_TOKENS = 128
VMEM_LIMIT = 56 * 1024 * 1024

F32 = jnp.float32
BF16 = jnp.bfloat16


def _dot(a, b):
    return jnp.dot(a, b, preferred_element_type=F32)


def _dot_nt(a, b):
    return lax.dot_general(a, b, (((1,), (1,)), ((), ())), preferred_element_type=F32)


def _dot_tn(a, b):
    return lax.dot_general(a, b, (((0,), (0,)), ((), ())), preferred_element_type=F32)


TOKEN_SLABS = 2
TOKEN_ROWS = D_MODEL // (TOKEN_SLABS * LANES)


def _token_shape(n_tokens):
    return (TOKEN_SLABS, n_tokens * TOKEN_ROWS, LANES)


def _token_chunks():
    for c in range(TOKEN_ROWS):
        for j in range(TOKEN_SLABS):
            yield j, c, slice((TOKEN_SLABS * c + j) * LANES, (TOKEN_SLABS * c + j + 1) * LANES)


def _store_token_tiles(ref, first_token, rows):
    n = rows.shape[0]
    for j, c, cols in _token_chunks():
        ref[j, pl.ds(first_token * TOKEN_ROWS + c, n, stride=TOKEN_ROWS), :] = rows[:, cols]


def _load_token_tiles(ref, n):
    return jnp.concatenate(
        [ref[j, pl.ds(c, n, stride=TOKEN_ROWS), :] for j, c, _ in _token_chunks()], axis=1)


def _rms(x, g):
    return x * lax.rsqrt(jnp.mean(x * x, axis=-1, keepdims=True) + NORM_EPS) * g


def _rope(col, cos, sin_signed, half):
    lane = lax.broadcasted_iota(jnp.int32, col.shape, 1)
    fwd = pltpu.roll(col, LANES - half, 1)
    bwd = pltpu.roll(col, half, 1)
    swapped = jnp.where((lane % (2 * half)) < half, fwd, bwd)
    return col * cos + swapped * sin_signed


def _in_proj(x, g_mix, w_in, b_in, cos64, sin64, cos128, sin128):
    h = _rms(x, g_mix).astype(BF16)
    z = _dot(h, w_in) + b_in

    def cols(off, width):
        return [z[:, off + c * LANES: off + (c + 1) * LANES] for c in range(width // LANES)]

    scale_q = ATT_HEAD_DIM ** -0.5
    aq = [_rope(c, cos64, sin64, ATT_HEAD_DIM // 2) * scale_q for c in cols(O_AQ, ATT_Q_W)]
    ak = _rope(z[:, O_AK:O_AK + LANES], cos64, sin64, ATT_HEAD_DIM // 2)
    av = z[:, O_AV:O_AV + LANES]
    rq = [_rope(c, cos128, sin128, RET_DK // 2) for c in cols(O_RQ, RET_W)]
    rk = [_rope(c, cos128, sin128, RET_DK // 2) * (RET_DK ** -0.5) for c in cols(O_RK, RET_W)]
    rv = cols(O_RV, RET_W)
    rg = z[:, O_RG:O_RG + RET_W]
    return aq, ak, av, rq, rk, rv, rg


def _expand_q(aq_cols, rows):
    lane = lax.broadcasted_iota(jnp.int32, (rows, LANES), 1)
    low = lane < ATT_HEAD_DIM
    pieces = []
    for h in range(ATT_HEADS):
        col, half, kv = aq_cols[h // 2], h % 2, h // ATT_GROUP
        if half != kv:
            col = pltpu.roll(col, ATT_HEAD_DIM, 1)
        pieces.append(jnp.where(low if kv == 0 else jnp.logical_not(low), col, 0.0))
    return jnp.concatenate(pieces, axis=0)


def _collect_heads(o, rows):
    lane = lax.broadcasted_iota(jnp.int32, (rows, LANES), 1)
    low = lane < ATT_HEAD_DIM
    out = []
    for c in range(ATT_HEADS // 2):
        pieces = []
        for half in (0, 1):
            h = 2 * c + half
            oh = o[h * rows:(h + 1) * rows]
            if h // ATT_GROUP != half:
                oh = pltpu.roll(oh, ATT_HEAD_DIM, 1)
            pieces.append(oh)
        out.append(jnp.where(low, pieces[0], pieces[1]))
    return jnp.concatenate(out, axis=1)


def _sink_softmax(s, mask, sink):
    s = jnp.where(mask, s, NEG_INF)
    m = jnp.maximum(jnp.max(s, axis=-1, keepdims=True), sink)
    e = jnp.exp(s - m)
    den = jnp.sum(e, axis=-1, keepdims=True) + jnp.exp(sink - m)
    return e * (1.0 / den)


def _group_norm_gate(ret, rg, g_ret):
    cols = []
    for h in range(RET_HEADS):
        of = ret[:, h * RET_DV:(h + 1) * RET_DV]
        d = of - jnp.mean(of, axis=-1, keepdims=True)
        var = jnp.mean(d * d, axis=-1, keepdims=True)
        cols.append(d * lax.rsqrt(var + GN_EPS))
    y = jnp.concatenate(cols, axis=1) * g_ret
    return y * (rg * jax.nn.sigmoid(rg))


def _route(x1, g_mlp, wr_hi, wr_lo, b_router, cnt):
    n = x1.shape[0]
    h2 = _rms(x1, g_mlp)
    hi = h2.astype(BF16)
    lo = (h2 - hi.astype(F32)).astype(BF16)
    logits = _dot(hi, wr_hi) + (_dot(hi, wr_lo) + _dot(lo, wr_hi)) + b_router
    work = logits.T[:N_EXPERTS]
    expert = lax.broadcasted_iota(jnp.int32, (N_EXPERTS, n), 0)
    idxs, vals = [], []
    for _ in range(TOP_K):
        m = jnp.max(work, axis=0, keepdims=True)
        idx = jnp.min(jnp.where(work == m, expert, N_EXPERTS), axis=0, keepdims=True)
        idxs.append(idx)
        vals.append(m)
        work = jnp.where(expert == idx, -jnp.inf, work)
    es = [jnp.exp(v - vals[0]) for v in vals]
    den = es[0] + es[1] + es[2] + es[3]
    gates = [e / den for e in es]

    sel = (expert == idxs[0]) | (expert == idxs[1]) | (expert == idxs[2]) | (expert == idxs[3])
    onehot = jnp.where(sel, 1.0, 0.0)
    r = lax.broadcasted_iota(jnp.int32, (n, n), 0)
    c = lax.broadcasted_iota(jnp.int32, (n, n), 1)
    earlier = jnp.where(r < c, 1.0, 0.0).astype(BF16)
    before = _dot(onehot.astype(BF16), earlier) + cnt
    row = lax.broadcasted_iota(jnp.int32, (LANES, n), 0)
    meta_t = jnp.zeros((LANES, n), F32)
    for k in range(TOP_K):
        rank = jnp.sum(jnp.where(expert == idxs[k], before, 0.0), axis=0, keepdims=True)
        meta_t = jnp.where(row == META_IDX + k, idxs[k].astype(F32), meta_t)
        meta_t = jnp.where(row == META_RANK + k, rank, meta_t)
        meta_t = jnp.where(row == META_GATE + k, gates[k], meta_t)
    return h2, meta_t.T, cnt + jnp.sum(onehot, axis=1, keepdims=True)


def _mixer_prompt_kernel(scal_ref, x_ref, cos64_ref, sin64_ref, cos128_ref, sin128_ref,
                         intra_ref, cross_ref, kdec_ref,
                         g_mix_ref, w_in_ref, b_in_ref, g_ret_ref, w_out_ref, b_out_ref,
                         g_mlp_ref, wr_hi_ref, wr_lo_ref, b_router_ref, h2s_ref, cnt_in_ref,
                         x1_ref, h2_hbm, meta_ref, cnt_out_ref, kp_ref, vp_ref, stp_ref,
                         prevk, prevv, st, cnt, h2buf, sems):
    b, n = pl.program_id(0), pl.program_id(1)
    step = b * pl.num_programs(1) + n
    last = pl.num_programs(0) * pl.num_programs(1) - 1
    block_rows = MIX_TOKENS * TOKEN_ROWS

    def h2_copy(s):
        rows = pl.ds(pl.multiple_of(s * block_rows, block_rows), block_rows)
        return pltpu.make_async_copy(h2buf, h2_hbm.at[:, rows], sems.at[0])

    @pl.when(step == 0)
    def _():
        cnt[...] = cnt_in_ref[...]

    @pl.when(n == 0)
    def _():
        prevk[...] = jnp.zeros_like(prevk)
        prevv[...] = jnp.zeros_like(prevv)
        st[...] = jnp.zeros_like(st)

    x = x_ref[...]
    aq, ak, av, rq, rk, rv, rg = _in_proj(
        x, g_mix_ref[...], w_in_ref[...], b_in_ref[...],
        cos64_ref[...], sin64_ref[...], cos128_ref[...], sin128_ref[...])

    row = lax.broadcasted_iota(jnp.int32, (BLK, 2 * BLK), 0)
    col = lax.broadcasted_iota(jnp.int32, (BLK, 2 * BLK), 1)
    mask_prev = (col < BLK) & (col > row)
    mask_cur = (col >= BLK) & (col - BLK <= row)

    k_prev, v_prev = prevk[...], prevv[...]
    mixed = []
    for j in range(MIX_TOKENS // BLK):
        sl = slice(j * BLK, (j + 1) * BLK)
        kk = jnp.concatenate([k_prev, ak[sl]], axis=0).astype(BF16)
        vv = jnp.concatenate([v_prev, av[sl]], axis=0).astype(BF16)
        qall = _expand_q([cq[sl] for cq in aq], BLK).astype(BF16)
        s = _dot_nt(qall, kk)
        mask = mask_cur | mask_prev if j > 0 else mask_cur | (mask_prev & (n > 0))
        probs = [_sink_softmax(s[h * BLK:(h + 1) * BLK], mask, scal_ref[h]) for h in range(ATT_HEADS)]
        o = _dot(jnp.concatenate(probs, axis=0).astype(BF16), vv)
        att = _collect_heads(o, BLK)
        k_prev, v_prev = ak[sl], av[sl]

        ret = []
        for h in range(RET_HEADS):
            qb, kh, vb = rq[h][sl].astype(BF16), rk[h][sl], rv[h][sl].astype(BF16)
            sc = _dot_nt(qb, kh.astype(BF16)) * intra_ref[h]
            sth = st[h]
            o_h = _dot(sc.astype(BF16), vb) + _dot(qb, sth.astype(BF16)) * cross_ref[h]
            st[h] = sth * scal_ref[ATT_HEADS + h] + _dot_tn((kh * kdec_ref[h]).astype(BF16), vb)
            ret.append(o_h)
        ret_out = _group_norm_gate(jnp.concatenate(ret, axis=1), rg[sl], g_ret_ref[...])
        mixed.append(jnp.concatenate([att, ret_out], axis=1))

    mixed = jnp.concatenate(mixed, axis=0).astype(BF16)
    x1 = x + _dot(mixed, w_out_ref[...]) + b_out_ref[...]
    x1_ref[...] = x1

    @pl.when(step > 0)
    def _():
        h2_copy(step - 1).wait()

    c = cnt[...]
    for j in range(MIX_TOKENS // BLK):
        sl = slice(j * BLK, (j + 1) * BLK)
        h2, meta, c = _route(x1[sl], g_mlp_ref[...], wr_hi_ref[...], wr_lo_ref[...], b_router_ref[...], c)
        _store_token_tiles(h2buf, j * BLK, h2)
        meta_ref[sl, :] = meta
    cnt[...] = c
    cnt_out_ref[...] = c
    h2_copy(step).start()

    @pl.when(step == last)
    def _():
        tail_rows = pl.ds((last + 1) * block_rows, h2s_ref.shape[1])
        tail = pltpu.make_async_copy(h2s_ref, h2_hbm.at[:, tail_rows], sems.at[1])
        tail.start()
        tail.wait()
        h2_copy(step).wait()

    prevk[...] = k_prev
    prevv[...] = v_prev
    kp_ref[0] = k_prev
    vp_ref[0] = v_prev
    stp_ref[0] = st[...]


def _const_spec(shape):
    nd = len(shape)
    return pl.BlockSpec(shape, lambda *_: (0,) * nd)


def _mixer_prompt(scal, x, tabs, ret_tabs, wts, h2_s, cnt_in):
    bsz, seq, _ = x.shape
    n_dec = h2_s.shape[1] // TOKEN_ROWS
    wts = tuple(wts) + (h2_s, cnt_in)
    steps = seq // MIX_TOKENS
    x2 = x.reshape(bsz * seq, D_MODEL)
    tok = lambda w: pl.BlockSpec((MIX_TOKENS, w), lambda b, n: (b * steps + n, 0))
    pos = lambda w: pl.BlockSpec((MIX_TOKENS, w), lambda b, n: (n, 0))
    in_specs = ([pl.BlockSpec(memory_space=pltpu.SMEM), tok(D_MODEL)]
                + [pos(LANES)] * 4
                + [_const_spec(t.shape) for t in ret_tabs]
                + [_const_spec(w.shape) for w in wts])
    out_shape = (
        jax.ShapeDtypeStruct((bsz * seq, D_MODEL), F32),
        jax.ShapeDtypeStruct(_token_shape(bsz * seq + n_dec), F32),
        jax.ShapeDtypeStruct((bsz * seq, LANES), F32),
        jax.ShapeDtypeStruct((N_EXPERTS, LANES), F32),
        jax.ShapeDtypeStruct((bsz, BLK, LANES), F32),
        jax.ShapeDtypeStruct((bsz, BLK, LANES), F32),
        jax.ShapeDtypeStruct((bsz, RET_HEADS, RET_DK, RET_DV), F32),
    )
    out_specs = (
        tok(D_MODEL), pl.BlockSpec(memory_space=pl.ANY), tok(LANES), _const_spec((N_EXPERTS, LANES)),
        pl.BlockSpec((1, BLK, LANES), lambda b, n: (b, 0, 0)),
        pl.BlockSpec((1, BLK, LANES), lambda b, n: (b, 0, 0)),
        pl.BlockSpec((1, RET_HEADS, RET_DK, RET_DV), lambda b, n: (b, 0, 0, 0)),
    )
    return pl.pallas_call(
        _mixer_prompt_kernel,
        grid=(bsz, steps),
        in_specs=in_specs,
        out_specs=out_specs,
        out_shape=out_shape,
        scratch_shapes=[pltpu.VMEM((BLK, LANES), F32), pltpu.VMEM((BLK, LANES), F32),
                        pltpu.VMEM((RET_HEADS, RET_DK, RET_DV), F32), pltpu.VMEM((N_EXPERTS, LANES), F32),
                        pltpu.VMEM(_token_shape(MIX_TOKENS), F32), pltpu.SemaphoreType.DMA((2,))],
        compiler_params=pltpu.CompilerParams(
            dimension_semantics=("arbitrary", "arbitrary"), vmem_limit_bytes=VMEM_LIMIT),
        name="mixer_prompt",
    )(scal, x2, *tabs, *ret_tabs, *wts)


def _sample_inproj_kernel(x_ref, cos64_ref, sin64_ref, cos128_ref, sin128_ref,
                          g_mix_ref, w_in_ref, b_in_ref,
                          aq_ref, ak_ref, av_ref, rq_ref, rk_ref, rv_ref, rg_ref):
    aq, ak, av, rq, rk, rv, rg = _in_proj(
        x_ref[...], g_mix_ref[...], w_in_ref[...], b_in_ref[...],
        cos64_ref[...], sin64_ref[...], cos128_ref[...], sin128_ref[...])
    aq_ref[...] = jnp.concatenate(aq, axis=1)
    ak_ref[...] = ak
    av_ref[...] = av
    rq_ref[...] = jnp.concatenate(rq, axis=1)
    rk_ref[...] = jnp.concatenate(rk, axis=1)
    rv_ref[...] = jnp.concatenate(rv, axis=1)
    rg_ref[...] = rg


def _sample_inproj(x, tabs, g_mix, w_in, b_in):
    n = x.shape[0]
    widths = (ATT_Q_W, LANES, LANES, RET_W, RET_W, RET_W, RET_W)
    args = (x, *tabs, g_mix, w_in, b_in)
    return pl.pallas_call(
        _sample_inproj_kernel,
        grid=(1,),
        in_specs=[_const_spec(a.shape) for a in args],
        out_specs=tuple(_const_spec((n, w)) for w in widths),
        out_shape=tuple(jax.ShapeDtypeStruct((n, w), F32) for w in widths),
        compiler_params=pltpu.CompilerParams(vmem_limit_bytes=VMEM_LIMIT),
        name="sample_inproj",
    )(*args)


def _sample_state_kernel(scal_ref, aq_ref, ak_ref, av_ref, rq_ref, rk_ref, rv_ref,
                         kbuf_ref, vbuf_ref, st_ref,
                         att_ref, ret_ref, kout_ref, vout_ref, stout_ref):
    nb = SAMPLE_CHUNK
    rows = nb * WINDOW

    def shift_in(buf_ref, new):
        flat = buf_ref[...].reshape(rows, LANES)
        rolled = pltpu.roll(flat, rows - 1, 0)
        rolled = rolled.reshape(nb, WINDOW, LANES)
        pos = lax.broadcasted_iota(jnp.int32, (nb, WINDOW, LANES), 1)
        return jnp.where(pos == WINDOW - 1, new[:, None, :], rolled)

    knew = shift_in(kbuf_ref, ak_ref[...])
    vnew = shift_in(vbuf_ref, av_ref[...])
    kout_ref[...] = knew
    vout_ref[...] = vnew

    aq = aq_ref[...]
    qall = _expand_q([aq[:, c * LANES:(c + 1) * LANES] for c in range(ATT_Q_W // LANES)], nb)
    s = _dot_nt(qall.astype(BF16), knew.reshape(rows, LANES).astype(BF16))
    r = lax.broadcasted_iota(jnp.int32, (nb, rows), 0)
    c = lax.broadcasted_iota(jnp.int32, (nb, rows), 1)
    own = (c // WINDOW) == r
    probs = [_sink_softmax(s[h * nb:(h + 1) * nb], own, scal_ref[h]) for h in range(ATT_HEADS)]
    o = _dot(jnp.concatenate(probs, axis=0).astype(BF16), vnew.reshape(rows, LANES).astype(BF16))
    att_ref[...] = _collect_heads(o, nb)

    qt = rq_ref[...].T
    kt = rk_ref[...].T
    rv = rv_ref[...]
    for bi in range(nb):
        for h in range(RET_HEADS):
            hs = slice(h * RET_DK, (h + 1) * RET_DK)
            s_new = st_ref[bi, h] * scal_ref[ATT_HEADS + h] + kt[hs, bi:bi + 1] * rv[bi:bi + 1, hs]
            stout_ref[bi, h] = s_new
            ret_ref[bi:bi + 1, hs] = jnp.sum(qt[hs, bi:bi + 1] * s_new, axis=0, keepdims=True)


def _sample_state(scal, aq, ak, av, rq, rk, rv, kbuf, vbuf, st):
    n = aq.shape[0]
    nb = SAMPLE_CHUNK
    row = lambda w: pl.BlockSpec((nb, w), lambda i: (i, 0))
    buf = pl.BlockSpec((nb, WINDOW, LANES), lambda i: (i, 0, 0))
    stt = pl.BlockSpec((nb, RET_HEADS, RET_DK, RET_DV), lambda i: (i, 0, 0, 0))
    return pl.pallas_call(
        _sample_state_kernel,
        grid=(n // nb,),
        in_specs=[pl.BlockSpec(memory_space=pltpu.SMEM), row(ATT_Q_W), row(LANES), row(LANES),
                  row(RET_W), row(RET_W), row(RET_W), buf, buf, stt],
        out_specs=(row(ATT_Q_W), row(RET_W), buf, buf, stt),
        out_shape=(jax.ShapeDtypeStruct((n, ATT_Q_W), F32), jax.ShapeDtypeStruct((n, RET_W), F32),
                   jax.ShapeDtypeStruct(kbuf.shape, F32), jax.ShapeDtypeStruct(vbuf.shape, F32),
                   jax.ShapeDtypeStruct(st.shape, F32)),
        compiler_params=pltpu.CompilerParams(
            dimension_semantics=("arbitrary",), vmem_limit_bytes=VMEM_LIMIT),
        name="sample_state",
    )(scal, aq, ak, av, rq, rk, rv, kbuf, vbuf, st)


def _sample_post_kernel(x_ref, att_ref, ret_ref, rg_ref, cnt_in_ref,
                        g_ret_ref, w_out_ref, b_out_ref, g_mlp_ref, wr_hi_ref, wr_lo_ref, b_router_ref,
                        x1_ref, h2_ref, meta_ref, cnt_out_ref):
    x = x_ref[...]
    ret_out = _group_norm_gate(ret_ref[...], rg_ref[...], g_ret_ref[...])
    mixed = jnp.concatenate([att_ref[...], ret_out], axis=1).astype(BF16)
    x1 = x + _dot(mixed, w_out_ref[...]) + b_out_ref[...]
    x1_ref[...] = x1
    h2, meta, c = _route(x1, g_mlp_ref[...], wr_hi_ref[...], wr_lo_ref[...], b_router_ref[...],
                         cnt_in_ref[...])
    _store_token_tiles(h2_ref, 0, h2)
    meta_ref[...] = meta
    cnt_out_ref[...] = c


def _sample_post(x, att, ret, rg, cnt, wts):
    n = x.shape[0]
    vm = [x, att, ret, rg, cnt, *wts]
    shapes = ((n, D_MODEL), _token_shape(n), (n, LANES), (N_EXPERTS, LANES))
    return pl.pallas_call(
        _sample_post_kernel,
        grid=(1,),
        in_specs=[_const_spec(a.shape) for a in vm],
        out_specs=tuple(_const_spec(s) for s in shapes),
        out_shape=tuple(jax.ShapeDtypeStruct(s, F32) for s in shapes),
        compiler_params=pltpu.CompilerParams(vmem_limit_bytes=VMEM_LIMIT),
        name="sample_post",
    )(*vm)


def _route_table_kernel(chunk, pos_hbm, words_hbm, defaults_hbm, table_hbm,
                        pos_smem, words_smem, table_smem, sem):
    preset = pltpu.make_async_copy(defaults_hbm, table_smem, sem)
    preset.start()
    preset.wait()
    for c in range(pos_hbm.shape[0] // chunk):
        part = pl.ds(c * chunk, chunk)
        for fetch in (pltpu.make_async_copy(pos_hbm.at[part], pos_smem, sem),
                      pltpu.make_async_copy(words_hbm.at[part], words_smem, sem)):
            fetch.start()
            fetch.wait()

        def body(g, carry):
            for u in range(TABLE_UNROLL):
                j = g * TABLE_UNROLL + u
                table_smem[pos_smem[j]] = words_smem[j]
            return carry

        lax.fori_loop(0, chunk // TABLE_UNROLL, body, 0)
    flush = pltpu.make_async_copy(table_smem, table_hbm, sem)
    flush.start()
    flush.wait()


def _route_table(pos, words, defaults):
    n = pos.shape[0]
    chunk = n // 4
    assert chunk * 4 == n and chunk % TABLE_UNROLL == 0
    any_spec = pl.BlockSpec(memory_space=pl.ANY)
    return pl.pallas_call(
        functools.partial(_route_table_kernel, chunk),
        grid=(1,),
        in_specs=[any_spec, any_spec, any_spec],
        out_specs=any_spec,
        out_shape=jax.ShapeDtypeStruct(defaults.shape, jnp.int32),
        scratch_shapes=[pltpu.SMEM((chunk,), jnp.int32), pltpu.SMEM((chunk,), jnp.int32),
                        pltpu.SMEM(defaults.shape, jnp.int32), pltpu.SemaphoreType.DMA],
        name="route_table",
    )(pos, words, defaults)


def _row_copy(src, src_token, dst, dst_token, sem):
    rows = lambda tok: pl.ds(pl.multiple_of(tok * TOKEN_ROWS, TOKEN_ROWS), TOKEN_ROWS)
    return pltpu.make_async_copy(src.at[:, rows(src_token)], dst.at[:, rows(dst_token)], sem)


def _for_each_row(n_rows, fn):
    def body(g, carry):
        for u in range(ROW_UNROLL):
            fn(g * ROW_UNROLL + u)
        return carry

    lax.fori_loop(0, n_rows // ROW_UNROLL, body, 0)


def _experts_kernel(spare_fills, v_expert, v_newexp, v_next, n_visits, table,
                    h2_hbm, w_in_hbm, b_in_ref, w_out_hbm, b_out_ref, slabs_hbm,
                    x_even, x_odd, y_even, y_odd, w_in_f32, w_out_f32, w_in_bf, w_out_bf,
                    sem_w, sem_x, sem_y):
    v = pl.program_id(0)
    total = n_visits[0]

    def weight_copies(e):
        return (pltpu.make_async_copy(w_in_hbm.at[e], w_in_f32, sem_w.at[0]),
                pltpu.make_async_copy(w_out_hbm.at[e], w_out_f32, sem_w.at[1]))

    def gather(word, r, x_buf, sem):
        return _row_copy(h2_hbm, word & ((1 << TABLE_SRC_BITS) - 1), x_buf, r, sem)

    def scatter(word, r, y_buf, sem):
        return _row_copy(y_buf, r, slabs_hbm, lax.shift_right_logical(word, TABLE_SRC_BITS), sem)

    def wait_gather(x_buf, sem):
        _for_each_row(EXPERT_ROWS, lambda r: gather(0, 0, x_buf, sem).wait())

    def wait_scatter(y_buf, sem):
        _for_each_row(EXPERT_ROWS, lambda r: scatter(0, 0, y_buf, sem).wait())

    @pl.when(v == 0)
    def _():
        y_odd[...] = jnp.zeros_like(y_odd)
        fill_rows = SPARE_FILL_TOKENS * TOKEN_ROWS
        fills = [pltpu.make_async_copy(y_odd.at[:, pl.ds(0, fill_rows)],
                                       slabs_hbm.at[:, pl.ds(tok * TOKEN_ROWS, fill_rows)], sem_y.at[1])
                 for tok in spare_fills]
        for cp in fills:
            cp.start()
        for cp in fills:
            cp.wait()
        _for_each_row(EXPERT_ROWS, lambda r: gather(table[r], r, x_even, sem_x.at[0]).start(priority=0))
        for cp in weight_copies(v_expert[0]):
            cp.start(priority=1)

    def visit(parity, x_buf, y_buf, x_next, y_prev):
        @pl.when(v_newexp[v] == 1)
        def _():
            for cp in weight_copies(v_expert[v]):
                cp.wait()
            w_in_bf[...] = w_in_f32[...].astype(BF16)
            w_out_bf[...] = w_out_f32[...].astype(BF16)

            @pl.when(v_next[v] >= 0)
            def _():
                for cp in weight_copies(v_next[v]):
                    cp.start(priority=1)

        wait_gather(x_buf, sem_x.at[parity])

        @pl.when(v >= 1)
        def _():
            wait_scatter(y_buf, sem_y.at[parity])

        next_row0 = jnp.minimum(v + 1, total - 1) * EXPERT_ROWS
        prev_row0 = jnp.maximum(v - 1, 0) * EXPERT_ROWS
        for r in range(EXPERT_ROWS):
            gather(table[next_row0 + r], r, x_next, sem_x.at[1 - parity]).start(priority=0)
            scatter(table[prev_row0 + r], r, y_prev, sem_y.at[1 - parity]).start(priority=1)

        x = _load_token_tiles(x_buf, EXPERT_ROWS).astype(BF16)
        hg = _dot(x, w_in_bf[...]) + b_in_ref[0]
        gate = jnp.minimum(hg[:, :D_FF], SWIGLU_LIMIT)
        lin = jnp.clip(hg[:, D_FF:], -SWIGLU_LIMIT, SWIGLU_LIMIT)
        act = gate * jax.nn.sigmoid(SWIGLU_ALPHA * gate) * (lin + 1.0)
        y = _dot(act.astype(BF16), w_out_bf[...]) + b_out_ref[0]
        _store_token_tiles(y_buf, 0, y)

        @pl.when(v == total - 1)
        def _():
            row0 = v * EXPERT_ROWS
            _for_each_row(EXPERT_ROWS,
                          lambda r: scatter(table[row0 + r], r, y_buf, sem_y.at[parity]).start(priority=1))
            wait_gather(x_next, sem_x.at[1 - parity])
            wait_scatter(y_prev, sem_y.at[1 - parity])
            wait_scatter(y_buf, sem_y.at[parity])

    @pl.when((v < total) & (v % 2 == 0))
    def _():
        visit(0, x_even, y_even, x_odd, y_odd)

    @pl.when((v < total) & (v % 2 == 1))
    def _():
        visit(1, x_odd, y_odd, x_even, y_even)


def _experts(visits, table, h2_all, slot_stride, w_e_in, b_e_in, w_e_out, b_e_out):
    n_tokens = h2_all.shape[1] // TOKEN_ROWS
    max_visits = table.shape[0] // EXPERT_ROWS
    any_spec = pl.BlockSpec(memory_space=pl.ANY)
    row_buf = pltpu.VMEM(_token_shape(EXPERT_ROWS), F32)
    grid_spec = pltpu.PrefetchScalarGridSpec(
        num_scalar_prefetch=len(visits) + 1,
        grid=(max_visits,),
        in_specs=[
            any_spec, any_spec,
            pl.BlockSpec((1, 1, 2 * D_FF), lambda v, ve, *_: (ve[v], 0, 0)),
            any_spec,
            pl.BlockSpec((1, 1, D_MODEL), lambda v, ve, *_: (ve[v], 0, 0)),
        ],
        out_specs=any_spec,
        scratch_shapes=[row_buf, row_buf, row_buf, row_buf,
                        pltpu.VMEM((D_MODEL, 2 * D_FF), F32), pltpu.VMEM((D_FF, D_MODEL), F32),
                        pltpu.VMEM((D_MODEL, 2 * D_FF), BF16), pltpu.VMEM((D_FF, D_MODEL), BF16),
                        pltpu.SemaphoreType.DMA((2,)), pltpu.SemaphoreType.DMA((2,)),
                        pltpu.SemaphoreType.DMA((2,))],
    )
    assert (slot_stride - n_tokens) % SPARE_FILL_TOKENS == 0
    spare_fills = tuple(k * slot_stride + n_tokens + i * SPARE_FILL_TOKENS
                        for k in range(TOP_K) for i in range((slot_stride - n_tokens) // SPARE_FILL_TOKENS))
    return pl.pallas_call(
        functools.partial(_experts_kernel, spare_fills),
        grid_spec=grid_spec,
        out_shape=jax.ShapeDtypeStruct(_token_shape(TOP_K * slot_stride), F32),
        compiler_params=pltpu.CompilerParams(
            dimension_semantics=("arbitrary",), vmem_limit_bytes=VMEM_LIMIT),
        name="experts",
    )(*visits, table, h2_all, w_e_in, b_e_in.reshape(N_EXPERTS, 1, 2 * D_FF), w_e_out,
      b_e_out.reshape(N_EXPERTS, 1, D_MODEL))


def _visit_schedule(tiles_e, max_visits):
    vis_end = jnp.cumsum(tiles_e)
    total = vis_end[-1]
    v = jnp.minimum(jnp.arange(max_visits, dtype=jnp.int32), total - 1)
    e = jnp.sum((vis_end[None, :] <= v[:, None]).astype(jnp.int32), axis=1)
    is_e = e[:, None] == jnp.arange(N_EXPERTS, dtype=jnp.int32)[None, :]
    pick = lambda table: jnp.sum(jnp.where(is_e, table[None, :], 0), axis=1)
    prev_e = jnp.concatenate([jnp.full((1,), -1, jnp.int32), e[:-1]])
    experts = jnp.arange(N_EXPERTS, dtype=jnp.int32)
    later = (experts[None, :] > experts[:, None]) & (tiles_e[None, :] > 0)
    next_e = jnp.min(jnp.where(later, experts[None, :], N_EXPERTS), axis=1)
    next_e = jnp.where(next_e == N_EXPERTS, -1, next_e)
    as_i32 = lambda a: a.astype(jnp.int32)
    return as_i32(e), as_i32(e != prev_e), as_i32(pick(next_e)), as_i32(total).reshape(1)


def _combine_kernel(tokens, x1_ref, meta_ref, p_ref, slab0_ref, slab1_ref, slab2_ref, slab3_ref,
                    g_ple_ref, w_pg_ref, w_ple_ref, g_final_ref, y_ref):
    meta = meta_ref[...]
    x1 = x1_ref[...]
    slabs = (slab0_ref, slab1_ref, slab2_ref, slab3_ref)
    gates = [jnp.broadcast_to(meta[:, META_GATE + k:META_GATE + k + 1], (tokens, LANES))
             for k in range(TOP_K)]
    cols = []
    for j, c, xcols in _token_chunks():
        acc = x1[:, xcols]
        for k in range(TOP_K):
            acc = acc + gates[k] * slabs[k][j, pl.ds(c, tokens, stride=TOKEN_ROWS), :]
        cols.append(acc)
    x2 = jnp.concatenate(cols, axis=1)
    hp = _rms(x2, g_ple_ref[...]).astype(BF16)
    gate = jax.nn.sigmoid(_dot(hp, w_pg_ref[...]))
    x3 = x2 + _dot(p_ref[...].astype(BF16), w_ple_ref[...]) * gate
    y_ref[...] = _rms(x3, g_final_ref[...])


def _combine(tokens, first_token, slot_stride, x1, meta, p, slabs, wts):
    n_out = x1.shape[0]
    assert first_token % tokens == 0 and slot_stride % tokens == 0
    tok = lambda w: pl.BlockSpec((tokens, w), lambda i: (i, 0))
    slot = lambda k: pl.BlockSpec(_token_shape(tokens),
                                  lambda i: (0, (k * slot_stride + first_token) // tokens + i, 0))
    return pl.pallas_call(
        functools.partial(_combine_kernel, tokens),
        grid=(n_out // tokens,),
        in_specs=[tok(D_MODEL), tok(LANES), tok(PLE_DIM)] + [slot(k) for k in range(TOP_K)]
                 + [_const_spec(w.shape) for w in wts],
        out_specs=tok(D_MODEL),
        out_shape=jax.ShapeDtypeStruct((n_out, D_MODEL), F32),
        compiler_params=pltpu.CompilerParams(
            dimension_semantics=("arbitrary",), vmem_limit_bytes=VMEM_LIMIT),
        name="combine_ple",
    )(x1, meta, p, slabs, slabs, slabs, slabs, *wts)


def _rope_tables(pos, n_rows):
    out = []
    for dim in (ATT_HEAD_DIM, RET_DK):
        half = dim // 2
        inv = ROPE_THETA ** (-np.arange(half, dtype=np.float64) / half)
        ang = np.asarray(pos, np.float64)[:, None] * inv[None, :]
        cos, sin = np.cos(ang), np.sin(ang)
        reps = LANES // dim
        cos_row = np.tile(np.concatenate([cos, cos], axis=-1), (1, reps))
        sin_row = np.tile(np.concatenate([-sin, sin], axis=-1), (1, reps))
        out += [jnp.asarray(np.broadcast_to(cos_row, (n_rows, LANES)), F32),
                jnp.asarray(np.broadcast_to(sin_row, (n_rows, LANES)), F32)]
    return out


def _retention_tables():
    log_g = np.log(1.0 - 2.0 ** (-5.0 - np.arange(RET_HEADS, dtype=np.float64)))
    t = np.arange(BLK, dtype=np.float64)
    rel = t[:, None] - t[None, :]
    intra = np.where(rel >= 0, np.exp(np.maximum(rel, 0.0)[None] * log_g[:, None, None]), 0.0)
    cross = np.exp((t + 1.0)[None, :] * log_g[:, None])
    kdec = np.exp((BLK - 1.0 - t)[None, :] * log_g[:, None])
    lane_bcast = lambda a: np.broadcast_to(a[:, :, None], (RET_HEADS, BLK, LANES))
    tabs = tuple(jnp.asarray(a, F32) for a in (intra, lane_bcast(cross), lane_bcast(kdec)))
    return log_g, tabs


def kernel(x_prompt, x_sample, state_swa_k, state_swa_v, state_ret, p_prompt, p_sample, g_mix, w_in, b_in,
           attn_sinks, g_ret_norm, w_out, b_out, g_mlp, w_router, b_router, w_e_in, b_e_in, w_e_out,
           b_e_out, g_ple, w_ple_gate, w_ple, g_final):
    bsz, seq, _ = x_prompt.shape
    n_dec = x_sample.shape[0]
    n_prompt = bsz * seq
    assert x_sample.shape[1] == 1 and g_mix.shape[0] == 1
    assert state_swa_k.shape[2] == WINDOW and n_prompt % n_dec == 0

    row = lambda a: a.reshape(1, -1)
    log_g, ret_tabs = _retention_tables()
    sinks = attn_sinks[0].astype(F32)
    scal_prompt = jnp.concatenate([sinks, jnp.asarray(np.exp(BLK * log_g), F32)])
    scal_sample = jnp.concatenate([sinks, jnp.asarray(np.exp(log_g), F32)])

    w_in_bf = w_in[0].astype(BF16)
    w_out_bf = w_out[0].astype(BF16)
    wr = jnp.pad(w_router[0], ((0, 0), (0, LANES - N_EXPERTS)))
    wr_hi = wr.astype(BF16)
    wr_lo = (wr - wr_hi.astype(F32)).astype(BF16)
    br = jnp.pad(b_router[0], (0, LANES - N_EXPERTS)).reshape(1, LANES)
    post_wts = (row(g_ret_norm[0]), w_out_bf, row(b_out[0]), row(g_mlp[0]), wr_hi, wr_lo, br)

    tabs_s = _rope_tables(np.full((1,), PAST_LEN), n_dec)
    xs2 = x_sample.reshape(n_dec, D_MODEL)
    aq, ak, av, rq, rk, rv, rg = _sample_inproj(xs2, tabs_s, row(g_mix[0]), w_in_bf, row(b_in[0]))
    att, ret, ks, vs, st_s = _sample_state(
        scal_sample, aq, ak, av, rq, rk, rv,
        state_swa_k[0].reshape(n_dec, WINDOW, LANES), state_swa_v[0].reshape(n_dec, WINDOW, LANES),
        state_ret[0])
    x1_s, h2_s, meta_s, cnt = _sample_post(xs2, att, ret, rg, jnp.zeros((N_EXPERTS, LANES), F32), post_wts)

    tabs_p = _rope_tables(np.arange(seq), seq)
    x1_p, h2_all, meta_p, cnt, kp, vp, st_p = _mixer_prompt(
        scal_prompt, x_prompt, tabs_p, ret_tabs, (row(g_mix[0]), w_in_bf, row(b_in[0])) + post_wts,
        h2_s, cnt)

    n_tokens = n_prompt + n_dec
    n_pairs = n_tokens * TOP_K
    max_visits = n_pairs // EXPERT_ROWS + N_EXPERTS
    slot_stride = -(-(n_tokens + 2 * EXPERT_ROWS) // EXPERT_ROWS) * EXPERT_ROWS
    assert n_tokens < (1 << TABLE_SRC_BITS) and TOP_K * slot_stride <= (1 << (32 - TABLE_SRC_BITS))
    counts = cnt[:, 0].astype(jnp.int32)
    tiles_e = (counts + EXPERT_ROWS - 1) // EXPERT_ROWS
    starts = (jnp.cumsum(tiles_e) - tiles_e) * EXPERT_ROWS

    def sorted_rows(meta):
        sel = meta[:, META_IDX:META_IDX + TOP_K].astype(jnp.int32)
        rank = meta[:, META_RANK:META_RANK + TOP_K].astype(jnp.int32)
        is_e = sel[:, :, None] == jnp.arange(N_EXPERTS, dtype=jnp.int32)[None, None, :]
        return (jnp.sum(jnp.where(is_e, starts[None, None, :], 0), axis=-1) + rank).reshape(-1)

    def pack(src_token, dst_slot):
        return lax.bitcast_convert_type(src_token.astype(jnp.uint32) | (dst_slot.astype(jnp.uint32) << TABLE_SRC_BITS),
                                        jnp.int32)

    pair = jnp.arange(n_pairs, dtype=jnp.int32)
    words = pack(pair // TOP_K, (pair % TOP_K) * slot_stride + pair // TOP_K)
    pad_row = jnp.arange(max_visits * EXPERT_ROWS, dtype=jnp.int32)
    spare = n_tokens + ((pad_row // EXPERT_ROWS) % 2) * EXPERT_ROWS + pad_row % EXPERT_ROWS
    table = _route_table(jnp.concatenate([sorted_rows(meta_p), sorted_rows(meta_s)]), words,
                         pack(jnp.zeros_like(pad_row), spare))

    visits = _visit_schedule(tiles_e, max_visits)
    slabs = _experts(visits, table, h2_all, slot_stride, w_e_in[0], b_e_in[0], w_e_out[0], b_e_out[0])

    ple_wts = (row(g_ple[0]), w_ple_gate[0].astype(BF16), w_ple[0].astype(BF16), row(g_final))
    y_p = _combine(COMBINE_TOKENS, 0, slot_stride, x1_p, meta_p, p_prompt[0].reshape(n_prompt, PLE_DIM),
                   slabs, ple_wts)
    y_s = _combine(n_dec, n_prompt, slot_stride, x1_s, meta_s, p_sample[0].reshape(n_dec, PLE_DIM),
                   slabs, ple_wts)

    kv_shape = (1, bsz, WINDOW, ATT_KV_HEADS, ATT_HEAD_DIM)
    dec_shape = (1, n_dec, WINDOW, ATT_KV_HEADS, ATT_HEAD_DIM)
    return (y_p.reshape(bsz, seq, D_MODEL), y_s.reshape(n_dec, 1, D_MODEL),
            kp.reshape(kv_shape), vp.reshape(kv_shape), st_p[None],
            ks.reshape(dec_shape), vs.reshape(dec_shape), st_s[None])
```

```python
import functools

import jax
import jax.numpy as jnp
import numpy as np
from jax import lax
from jax.experimental import pallas as pl
from jax.experimental.pallas import tpu as pltpu

D_MODEL = 1024
PAST_LEN = 16384
PLE_DIM = 256
ATT_HEADS = 8
ATT_KV_HEADS = 2
ATT_HEAD_DIM = 64
ATT_GROUP = ATT_HEADS // ATT_KV_HEADS
WINDOW = 128
BLK = 128
RET_HEADS = 4
RET_DK = 128
RET_DV = 128
ROPE_THETA = 10000.0
N_EXPERTS = 32
TOP_K = 4
D_FF = 1024
SWIGLU_LIMIT = 7.0
SWIGLU_ALPHA = 1.702
NORM_EPS = 1e-5
GN_EPS = 1e-6
NEG_INF = -1e30

ATT_Q_W = ATT_HEADS * ATT_HEAD_DIM
ATT_KV_W = ATT_KV_HEADS * ATT_HEAD_DIM
RET_W = RET_HEADS * RET_DK
IN_W = ATT_Q_W + 2 * ATT_KV_W + 4 * RET_W
O_AQ = 0
O_AK = ATT_Q_W
O_AV = O_AK + ATT_KV_W
O_RQ = O_AV + ATT_KV_W
O_RK = O_RQ + RET_W
O_RV = O_RK + RET_W
O_RG = O_RV + RET_W

LANES = 128
META_IDX, META_RANK, META_GATE = 0, 4, 8

MIX_TOKENS = 1024
DISPATCH_TOKENS = 256
EXPERT_ROWS = 256
COMBINE_TOKENS = 256
SAMPLE_CHUNK = 16
ROW_UNROLL = 4
VMEM_LIMIT = 56 * 1024 * 1024

F32 = jnp.float32
BF16 = jnp.bfloat16


def _dot(a, b):
    return jnp.dot(a, b, preferred_element_type=F32)


def _dot_nt(a, b):
    return lax.dot_general(a, b, (((1,), (1,)), ((), ())), preferred_element_type=F32)


def _dot_tn(a, b):
    return lax.dot_general(a, b, (((0,), (0,)), ((), ())), preferred_element_type=F32)


TOKEN_SLABS = 2
TOKEN_ROWS = D_MODEL // (TOKEN_SLABS * LANES)


def _token_shape(n_tokens):
    return (TOKEN_SLABS, n_tokens * TOKEN_ROWS, LANES)


def _token_chunks():
    for c in range(TOKEN_ROWS):
        for j in range(TOKEN_SLABS):
            yield j, c, slice((TOKEN_SLABS * c + j) * LANES, (TOKEN_SLABS * c + j + 1) * LANES)


def _store_token_tiles(ref, first_token, rows):
    n = rows.shape[0]
    for j, c, cols in _token_chunks():
        ref[j, pl.ds(first_token * TOKEN_ROWS + c, n, stride=TOKEN_ROWS), :] = rows[:, cols]


def _load_token_tiles(ref, n):
    return jnp.concatenate(
        [ref[j, pl.ds(c, n, stride=TOKEN_ROWS), :] for j, c, _ in _token_chunks()], axis=1)


def _rms(x, g):
    return x * lax.rsqrt(jnp.mean(x * x, axis=-1, keepdims=True) + NORM_EPS) * g


def _rope(col, cos, sin_signed, half):
    lane = lax.broadcasted_iota(jnp.int32, col.shape, 1)
    fwd = pltpu.roll(col, LANES - half, 1)
    bwd = pltpu.roll(col, half, 1)
    swapped = jnp.where((lane % (2 * half)) < half, fwd, bwd)
    return col * cos + swapped * sin_signed


def _in_proj(x, g_mix, w_in, b_in, cos64, sin64, cos128, sin128):
    h = _rms(x, g_mix).astype(BF16)
    z = _dot(h, w_in) + b_in

    def cols(off, width):
        return [z[:, off + c * LANES: off + (c + 1) * LANES] for c in range(width // LANES)]

    scale_q = ATT_HEAD_DIM ** -0.5
    aq = [_rope(c, cos64, sin64, ATT_HEAD_DIM // 2) * scale_q for c in cols(O_AQ, ATT_Q_W)]
    ak = _rope(z[:, O_AK:O_AK + LANES], cos64, sin64, ATT_HEAD_DIM // 2)
    av = z[:, O_AV:O_AV + LANES]
    rq = [_rope(c, cos128, sin128, RET_DK // 2) for c in cols(O_RQ, RET_W)]
    rk = [_rope(c, cos128, sin128, RET_DK // 2) * (RET_DK ** -0.5) for c in cols(O_RK, RET_W)]
    rv = cols(O_RV, RET_W)
    rg = z[:, O_RG:O_RG + RET_W]
    return aq, ak, av, rq, rk, rv, rg


def _expand_q(aq_cols, rows):
    lane = lax.broadcasted_iota(jnp.int32, (rows, LANES), 1)
    low = lane < ATT_HEAD_DIM
    pieces = []
    for h in range(ATT_HEADS):
        col, half, kv = aq_cols[h // 2], h % 2, h // ATT_GROUP
        if half != kv:
            col = pltpu.roll(col, ATT_HEAD_DIM, 1)
        pieces.append(jnp.where(low if kv == 0 else jnp.logical_not(low), col, 0.0))
    return jnp.concatenate(pieces, axis=0)


def _collect_heads(o, rows):
    lane = lax.broadcasted_iota(jnp.int32, (rows, LANES), 1)
    low = lane < ATT_HEAD_DIM
    out = []
    for c in range(ATT_HEADS // 2):
        pieces = []
        for half in (0, 1):
            h = 2 * c + half
            oh = o[h * rows:(h + 1) * rows]
            if h // ATT_GROUP != half:
                oh = pltpu.roll(oh, ATT_HEAD_DIM, 1)
            pieces.append(oh)
        out.append(jnp.where(low, pieces[0], pieces[1]))
    return jnp.concatenate(out, axis=1)


def _sink_softmax(s, mask, sink):
    s = jnp.where(mask, s, NEG_INF)
    m = jnp.maximum(jnp.max(s, axis=-1, keepdims=True), sink)
    e = jnp.exp(s - m)
    den = jnp.sum(e, axis=-1, keepdims=True) + jnp.exp(sink - m)
    return e * (1.0 / den)


def _group_norm_gate(ret, rg, g_ret):
    cols = []
    for h in range(RET_HEADS):
        of = ret[:, h * RET_DV:(h + 1) * RET_DV]
        d = of - jnp.mean(of, axis=-1, keepdims=True)
        var = jnp.mean(d * d, axis=-1, keepdims=True)
        cols.append(d * lax.rsqrt(var + GN_EPS))
    y = jnp.concatenate(cols, axis=1) * g_ret
    return y * (rg * jax.nn.sigmoid(rg))


def _route(x1, g_mlp, wr_hi, wr_lo, b_router, cnt):
    n = x1.shape[0]
    h2 = _rms(x1, g_mlp)
    hi = h2.astype(BF16)
    lo = (h2 - hi.astype(F32)).astype(BF16)
    logits = _dot(hi, wr_hi) + (_dot(hi, wr_lo) + _dot(lo, wr_hi)) + b_router
    work = logits.T[:N_EXPERTS]
    expert = lax.broadcasted_iota(jnp.int32, (N_EXPERTS, n), 0)
    idxs, vals = [], []
    for _ in range(TOP_K):
        m = jnp.max(work, axis=0, keepdims=True)
        idx = jnp.min(jnp.where(work == m, expert, N_EXPERTS), axis=0, keepdims=True)
        idxs.append(idx)
        vals.append(m)
        work = jnp.where(expert == idx, -jnp.inf, work)
    es = [jnp.exp(v - vals[0]) for v in vals]
    den = es[0] + es[1] + es[2] + es[3]
    gates = [e / den for e in es]

    sel = (expert == idxs[0]) | (expert == idxs[1]) | (expert == idxs[2]) | (expert == idxs[3])
    onehot = jnp.where(sel, 1.0, 0.0)
    r = lax.broadcasted_iota(jnp.int32, (n, n), 0)
    c = lax.broadcasted_iota(jnp.int32, (n, n), 1)
    earlier = jnp.where(r < c, 1.0, 0.0).astype(BF16)
    before = _dot(onehot.astype(BF16), earlier) + cnt
    row = lax.broadcasted_iota(jnp.int32, (LANES, n), 0)
    meta_t = jnp.zeros((LANES, n), F32)
    for k in range(TOP_K):
        rank = jnp.sum(jnp.where(expert == idxs[k], before, 0.0), axis=0, keepdims=True)
        meta_t = jnp.where(row == META_IDX + k, idxs[k].astype(F32), meta_t)
        meta_t = jnp.where(row == META_RANK + k, rank, meta_t)
        meta_t = jnp.where(row == META_GATE + k, gates[k], meta_t)
    return h2, meta_t.T, cnt + jnp.sum(onehot, axis=1, keepdims=True)


def _mixer_prompt_kernel(scal_ref, x_ref, cos64_ref, sin64_ref, cos128_ref, sin128_ref,
                         intra_ref, cross_ref, kdec_ref,
                         g_mix_ref, w_in_ref, b_in_ref, g_ret_ref, w_out_ref, b_out_ref,
                         g_mlp_ref, wr_hi_ref, wr_lo_ref, b_router_ref,
                         x1_ref, h2_ref, meta_ref, cnt_out_ref, kp_ref, vp_ref, stp_ref,
                         prevk, prevv, st, cnt):
    b, n = pl.program_id(0), pl.program_id(1)

    @pl.when((b == 0) & (n == 0))
    def _():
        cnt[...] = jnp.zeros_like(cnt)

    @pl.when(n == 0)
    def _():
        prevk[...] = jnp.zeros_like(prevk)
        prevv[...] = jnp.zeros_like(prevv)
        st[...] = jnp.zeros_like(st)

    x = x_ref[...]
    aq, ak, av, rq, rk, rv, rg = _in_proj(
        x, g_mix_ref[...], w_in_ref[...], b_in_ref[...],
        cos64_ref[...], sin64_ref[...], cos128_ref[...], sin128_ref[...])

    row = lax.broadcasted_iota(jnp.int32, (BLK, 2 * BLK), 0)
    col = lax.broadcasted_iota(jnp.int32, (BLK, 2 * BLK), 1)
    mask_prev = (col < BLK) & (col > row)
    mask_cur = (col >= BLK) & (col - BLK <= row)

    k_prev, v_prev = prevk[...], prevv[...]
    mixed = []
    for j in range(MIX_TOKENS // BLK):
        sl = slice(j * BLK, (j + 1) * BLK)
        kk = jnp.concatenate([k_prev, ak[sl]], axis=0).astype(BF16)
        vv = jnp.concatenate([v_prev, av[sl]], axis=0).astype(BF16)
        qall = _expand_q([cq[sl] for cq in aq], BLK).astype(BF16)
        s = _dot_nt(qall, kk)
        mask = mask_cur | mask_prev if j > 0 else mask_cur | (mask_prev & (n > 0))
        probs = [_sink_softmax(s[h * BLK:(h + 1) * BLK], mask, scal_ref[h]) for h in range(ATT_HEADS)]
        o = _dot(jnp.concatenate(probs, axis=0).astype(BF16), vv)
        att = _collect_heads(o, BLK)
        k_prev, v_prev = ak[sl], av[sl]

        ret = []
        for h in range(RET_HEADS):
            qb, kh, vb = rq[h][sl].astype(BF16), rk[h][sl], rv[h][sl].astype(BF16)
            sc = _dot_nt(qb, kh.astype(BF16)) * intra_ref[h]
            sth = st[h]
            o_h = _dot(sc.astype(BF16), vb) + _dot(qb, sth.astype(BF16)) * cross_ref[h]
            st[h] = sth * scal_ref[ATT_HEADS + h] + _dot_tn((kh * kdec_ref[h]).astype(BF16), vb)
            ret.append(o_h)
        ret_out = _group_norm_gate(jnp.concatenate(ret, axis=1), rg[sl], g_ret_ref[...])
        mixed.append(jnp.concatenate([att, ret_out], axis=1))

    mixed = jnp.concatenate(mixed, axis=0).astype(BF16)
    x1 = x + _dot(mixed, w_out_ref[...]) + b_out_ref[...]
    x1_ref[...] = x1

    c = cnt[...]
    for j in range(MIX_TOKENS // BLK):
        sl = slice(j * BLK, (j + 1) * BLK)
        h2, meta, c = _route(x1[sl], g_mlp_ref[...], wr_hi_ref[...], wr_lo_ref[...], b_router_ref[...], c)
        _store_token_tiles(h2_ref, j * BLK, h2)
        meta_ref[sl, :] = meta
    cnt[...] = c
    cnt_out_ref[...] = c

    prevk[...] = k_prev
    prevv[...] = v_prev
    kp_ref[0] = k_prev
    vp_ref[0] = v_prev
    stp_ref[0] = st[...]


def _const_spec(shape):
    nd = len(shape)
    return pl.BlockSpec(shape, lambda *_: (0,) * nd)


def _mixer_prompt(scal, x, tabs, ret_tabs, wts):
    bsz, seq, _ = x.shape
    steps = seq // MIX_TOKENS
    x2 = x.reshape(bsz * seq, D_MODEL)
    tok = lambda w: pl.BlockSpec((MIX_TOKENS, w), lambda b, n: (b * steps + n, 0))
    pos = lambda w: pl.BlockSpec((MIX_TOKENS, w), lambda b, n: (n, 0))
    in_specs = ([pl.BlockSpec(memory_space=pltpu.SMEM), tok(D_MODEL)]
                + [pos(LANES)] * 4
                + [_const_spec(t.shape) for t in ret_tabs]
                + [_const_spec(w.shape) for w in wts])
    out_shape = (
        jax.ShapeDtypeStruct((bsz * seq, D_MODEL), F32),
        jax.ShapeDtypeStruct(_token_shape(bsz * seq), F32),
        jax.ShapeDtypeStruct((bsz * seq, LANES), F32),
        jax.ShapeDtypeStruct((N_EXPERTS, LANES), F32),
        jax.ShapeDtypeStruct((bsz, BLK, LANES), F32),
        jax.ShapeDtypeStruct((bsz, BLK, LANES), F32),
        jax.ShapeDtypeStruct((bsz, RET_HEADS, RET_DK, RET_DV), F32),
    )
    out_specs = (
        tok(D_MODEL), pl.BlockSpec(_token_shape(MIX_TOKENS), lambda b, n: (0, b * steps + n, 0)),
        tok(LANES), _const_spec((N_EXPERTS, LANES)),
        pl.BlockSpec((1, BLK, LANES), lambda b, n: (b, 0, 0)),
        pl.BlockSpec((1, BLK, LANES), lambda b, n: (b, 0, 0)),
        pl.BlockSpec((1, RET_HEADS, RET_DK, RET_DV), lambda b, n: (b, 0, 0, 0)),
    )
    return pl.pallas_call(
        _mixer_prompt_kernel,
        grid=(bsz, steps),
        in_specs=in_specs,
        out_specs=out_specs,
        out_shape=out_shape,
        scratch_shapes=[pltpu.VMEM((BLK, LANES), F32), pltpu.VMEM((BLK, LANES), F32),
                        pltpu.VMEM((RET_HEADS, RET_DK, RET_DV), F32), pltpu.VMEM((N_EXPERTS, LANES), F32)],
        compiler_params=pltpu.CompilerParams(
            dimension_semantics=("arbitrary", "arbitrary"), vmem_limit_bytes=VMEM_LIMIT),
        name="mixer_prompt",
    )(scal, x2, *tabs, *ret_tabs, *wts)


def _sample_inproj_kernel(x_ref, cos64_ref, sin64_ref, cos128_ref, sin128_ref,
                          g_mix_ref, w_in_ref, b_in_ref,
                          aq_ref, ak_ref, av_ref, rq_ref, rk_ref, rv_ref, rg_ref):
    aq, ak, av, rq, rk, rv, rg = _in_proj(
        x_ref[...], g_mix_ref[...], w_in_ref[...], b_in_ref[...],
        cos64_ref[...], sin64_ref[...], cos128_ref[...], sin128_ref[...])
    aq_ref[...] = jnp.concatenate(aq, axis=1)
    ak_ref[...] = ak
    av_ref[...] = av
    rq_ref[...] = jnp.concatenate(rq, axis=1)
    rk_ref[...] = jnp.concatenate(rk, axis=1)
    rv_ref[...] = jnp.concatenate(rv, axis=1)
    rg_ref[...] = rg


def _sample_inproj(x, tabs, g_mix, w_in, b_in):
    n = x.shape[0]
    widths = (ATT_Q_W, LANES, LANES, RET_W, RET_W, RET_W, RET_W)
    args = (x, *tabs, g_mix, w_in, b_in)
    return pl.pallas_call(
        _sample_inproj_kernel,
        grid=(1,),
        in_specs=[_const_spec(a.shape) for a in args],
        out_specs=tuple(_const_spec((n, w)) for w in widths),
        out_shape=tuple(jax.ShapeDtypeStruct((n, w), F32) for w in widths),
        compiler_params=pltpu.CompilerParams(vmem_limit_bytes=VMEM_LIMIT),
        name="sample_inproj",
    )(*args)


def _sample_state_kernel(scal_ref, aq_ref, ak_ref, av_ref, rq_ref, rk_ref, rv_ref,
                         kbuf_ref, vbuf_ref, st_ref,
                         att_ref, ret_ref, kout_ref, vout_ref, stout_ref):
    nb = SAMPLE_CHUNK
    rows = nb * WINDOW

    def shift_in(buf_ref, new):
        flat = buf_ref[...].reshape(rows, LANES)
        rolled = pltpu.roll(flat, rows - 1, 0)
        rolled = rolled.reshape(nb, WINDOW, LANES)
        pos = lax.broadcasted_iota(jnp.int32, (nb, WINDOW, LANES), 1)
        return jnp.where(pos == WINDOW - 1, new[:, None, :], rolled)

    knew = shift_in(kbuf_ref, ak_ref[...])
    vnew = shift_in(vbuf_ref, av_ref[...])
    kout_ref[...] = knew
    vout_ref[...] = vnew

    aq = aq_ref[...]
    qall = _expand_q([aq[:, c * LANES:(c + 1) * LANES] for c in range(ATT_Q_W // LANES)], nb)
    s = _dot_nt(qall.astype(BF16), knew.reshape(rows, LANES).astype(BF16))
    r = lax.broadcasted_iota(jnp.int32, (nb, rows), 0)
    c = lax.broadcasted_iota(jnp.int32, (nb, rows), 1)
    own = (c // WINDOW) == r
    probs = [_sink_softmax(s[h * nb:(h + 1) * nb], own, scal_ref[h]) for h in range(ATT_HEADS)]
    o = _dot(jnp.concatenate(probs, axis=0).astype(BF16), vnew.reshape(rows, LANES).astype(BF16))
    att_ref[...] = _collect_heads(o, nb)

    qt = rq_ref[...].T
    kt = rk_ref[...].T
    rv = rv_ref[...]
    for bi in range(nb):
        for h in range(RET_HEADS):
            hs = slice(h * RET_DK, (h + 1) * RET_DK)
            s_new = st_ref[bi, h] * scal_ref[ATT_HEADS + h] + kt[hs, bi:bi + 1] * rv[bi:bi + 1, hs]
            stout_ref[bi, h] = s_new
            ret_ref[bi:bi + 1, hs] = jnp.sum(qt[hs, bi:bi + 1] * s_new, axis=0, keepdims=True)


def _sample_state(scal, aq, ak, av, rq, rk, rv, kbuf, vbuf, st):
    n = aq.shape[0]
    nb = SAMPLE_CHUNK
    row = lambda w: pl.BlockSpec((nb, w), lambda i: (i, 0))
    buf = pl.BlockSpec((nb, WINDOW, LANES), lambda i: (i, 0, 0))
    stt = pl.BlockSpec((nb, RET_HEADS, RET_DK, RET_DV), lambda i: (i, 0, 0, 0))
    return pl.pallas_call(
        _sample_state_kernel,
        grid=(n // nb,),
        in_specs=[pl.BlockSpec(memory_space=pltpu.SMEM), row(ATT_Q_W), row(LANES), row(LANES),
                  row(RET_W), row(RET_W), row(RET_W), buf, buf, stt],
        out_specs=(row(ATT_Q_W), row(RET_W), buf, buf, stt),
        out_shape=(jax.ShapeDtypeStruct((n, ATT_Q_W), F32), jax.ShapeDtypeStruct((n, RET_W), F32),
                   jax.ShapeDtypeStruct(kbuf.shape, F32), jax.ShapeDtypeStruct(vbuf.shape, F32),
                   jax.ShapeDtypeStruct(st.shape, F32)),
        compiler_params=pltpu.CompilerParams(
            dimension_semantics=("arbitrary",), vmem_limit_bytes=VMEM_LIMIT),
        name="sample_state",
    )(scal, aq, ak, av, rq, rk, rv, kbuf, vbuf, st)


def _sample_post_kernel(x_ref, att_ref, ret_ref, rg_ref, cnt_in_ref,
                        g_ret_ref, w_out_ref, b_out_ref, g_mlp_ref, wr_hi_ref, wr_lo_ref, b_router_ref,
                        x1_ref, h2_ref, meta_ref, cnt_out_ref):
    x = x_ref[...]
    ret_out = _group_norm_gate(ret_ref[...], rg_ref[...], g_ret_ref[...])
    mixed = jnp.concatenate([att_ref[...], ret_out], axis=1).astype(BF16)
    x1 = x + _dot(mixed, w_out_ref[...]) + b_out_ref[...]
    x1_ref[...] = x1
    h2, meta, c = _route(x1, g_mlp_ref[...], wr_hi_ref[...], wr_lo_ref[...], b_router_ref[...],
                         cnt_in_ref[...])
    _store_token_tiles(h2_ref, 0, h2)
    meta_ref[...] = meta
    cnt_out_ref[...] = c


def _sample_post(x, att, ret, rg, cnt, wts):
    n = x.shape[0]
    vm = [x, att, ret, rg, cnt, *wts]
    shapes = ((n, D_MODEL), _token_shape(n), (n, LANES), (N_EXPERTS, LANES))
    return pl.pallas_call(
        _sample_post_kernel,
        grid=(1,),
        in_specs=[_const_spec(a.shape) for a in vm],
        out_specs=tuple(_const_spec(s) for s in shapes),
        out_shape=tuple(jax.ShapeDtypeStruct(s, F32) for s in shapes),
        compiler_params=pltpu.CompilerParams(vmem_limit_bytes=VMEM_LIMIT),
        name="sample_post",
    )(*vm)


def _row_copy(src, src_token, dst, dst_token, sem):
    rows = lambda tok: pl.ds(pl.multiple_of(tok * TOKEN_ROWS, TOKEN_ROWS), TOKEN_ROWS)
    return pltpu.make_async_copy(src.at[:, rows(src_token)], dst.at[:, rows(dst_token)], sem)


def _for_each_row(n_tok, fn):
    def body(g, carry):
        for u in range(ROW_UNROLL):
            for k in range(TOP_K):
                fn(g * ROW_UNROLL + u, k)
        return carry

    lax.fori_loop(0, n_tok // ROW_UNROLL, body, 0)


def _dispatch_kernel(prompt_steps, n_dec, pos_p, pos_s, h2p_ref, h2s_ref, xs_hbm, sem_rows):
    i = pl.program_id(0)

    def scatter_rows(src_ref, n_tok, pos_ref, first_idx):
        def copy(t, k):
            return _row_copy(src_ref, t, xs_hbm, pos_ref[first_idx + t * TOP_K + k], sem_rows)

        _for_each_row(n_tok, lambda t, k: copy(t, k).start(priority=k % 2))
        _for_each_row(n_tok, lambda t, k: _row_copy(src_ref, 0, xs_hbm, 0, sem_rows).wait())

    @pl.when(i < prompt_steps)
    def _():
        scatter_rows(h2p_ref, DISPATCH_TOKENS, pos_p, i * (DISPATCH_TOKENS * TOP_K))

    @pl.when(i == prompt_steps)
    def _():
        scatter_rows(h2s_ref, n_dec, pos_s, 0)


def _dispatch(pos_p, pos_s, h2_p, h2_s):
    n_prompt, n_dec = h2_p.shape[1] // TOKEN_ROWS, h2_s.shape[1] // TOKEN_ROWS
    prompt_steps = n_prompt // DISPATCH_TOKENS
    grid_spec = pltpu.PrefetchScalarGridSpec(
        num_scalar_prefetch=2,
        grid=(prompt_steps + 1,),
        in_specs=[pl.BlockSpec(_token_shape(DISPATCH_TOKENS),
                               lambda i, *_: (0, jnp.minimum(i, prompt_steps - 1), 0)),
                  pl.BlockSpec(h2_s.shape, lambda i, *_: (0, 0, 0))],
        out_specs=pl.BlockSpec(memory_space=pl.ANY),
        scratch_shapes=[pltpu.SemaphoreType.DMA],
    )
    return pl.pallas_call(
        functools.partial(_dispatch_kernel, prompt_steps, n_dec),
        grid_spec=grid_spec,
        out_shape=jax.ShapeDtypeStruct(_token_shape((n_prompt + n_dec) * TOP_K), F32),
        compiler_params=pltpu.CompilerParams(
            dimension_semantics=("arbitrary",), vmem_limit_bytes=VMEM_LIMIT),
        name="dispatch",
    )(pos_p, pos_s, h2_p, h2_s)


def _experts_kernel(v_tile, v_expert, v_lo, v_hi, v_newtile, v_newexp, v_next, n_visits,
                    xs_ref, w_in_hbm, b_in_ref, w_out_hbm, b_out_ref, y_ref,
                    w_in_f32, w_out_f32, w_in_bf, w_out_bf, sems):
    v = pl.program_id(0)

    def weight_copies(e):
        return (pltpu.make_async_copy(w_in_hbm.at[e], w_in_f32, sems.at[0]),
                pltpu.make_async_copy(w_out_hbm.at[e], w_out_f32, sems.at[1]))

    @pl.when(v < n_visits[0])
    def _():
        @pl.when(v_newexp[v] == 1)
        def _():
            @pl.when(v == 0)
            def _():
                for cp in weight_copies(v_expert[0]):
                    cp.start(priority=1)

            for cp in weight_copies(v_expert[v]):
                cp.wait()
            w_in_bf[...] = w_in_f32[...].astype(BF16)
            w_out_bf[...] = w_out_f32[...].astype(BF16)

            @pl.when(v_next[v] >= 0)
            def _():
                for cp in weight_copies(v_next[v]):
                    cp.start(priority=1)

        x = _load_token_tiles(xs_ref, EXPERT_ROWS).astype(BF16)
        hg = _dot(x, w_in_bf[...]) + b_in_ref[0]
        gate = jnp.minimum(hg[:, :D_FF], SWIGLU_LIMIT)
        lin = jnp.clip(hg[:, D_FF:], -SWIGLU_LIMIT, SWIGLU_LIMIT)
        act = gate * jax.nn.sigmoid(SWIGLU_ALPHA * gate) * (lin + 1.0)
        y = _dot(act.astype(BF16), w_out_bf[...]) + b_out_ref[0]

        row = lax.broadcasted_iota(jnp.int32, (EXPERT_ROWS, LANES), 0)
        mine = (row >= v_lo[v]) & (row < v_hi[v])

        def write_rows(first_visit_of_tile):
            for j, c, cols in _token_chunks():
                rows_c = pl.ds(c, EXPERT_ROWS, stride=TOKEN_ROWS)
                keep = 0.0 if first_visit_of_tile else y_ref[j, rows_c, :]
                y_ref[j, rows_c, :] = jnp.where(mine, y[:, cols], keep)

        @pl.when(v_newtile[v] == 1)
        def _():
            write_rows(True)

        @pl.when(v_newtile[v] == 0)
        def _():
            write_rows(False)


def _experts(visits, xs, w_e_in, b_e_in, w_e_out, b_e_out):
    n_rows = xs.shape[1] // TOKEN_ROWS
    max_visits = n_rows // EXPERT_ROWS + N_EXPERTS - 1
    tile_rows = pl.BlockSpec(_token_shape(EXPERT_ROWS), lambda v, vt, *_: (0, vt[v], 0))
    any_spec = pl.BlockSpec(memory_space=pl.ANY)
    grid_spec = pltpu.PrefetchScalarGridSpec(
        num_scalar_prefetch=len(visits),
        grid=(max_visits,),
        in_specs=[
            tile_rows, any_spec,
            pl.BlockSpec((1, 1, 2 * D_FF), lambda v, vt, ve, *_: (ve[v], 0, 0)),
            any_spec,
            pl.BlockSpec((1, 1, D_MODEL), lambda v, vt, ve, *_: (ve[v], 0, 0)),
        ],
        out_specs=tile_rows,
        scratch_shapes=[pltpu.VMEM((D_MODEL, 2 * D_FF), F32), pltpu.VMEM((D_FF, D_MODEL), F32),
                        pltpu.VMEM((D_MODEL, 2 * D_FF), BF16), pltpu.VMEM((D_FF, D_MODEL), BF16),
                        pltpu.SemaphoreType.DMA((2,))],
    )
    return pl.pallas_call(
        _experts_kernel,
        grid_spec=grid_spec,
        out_shape=jax.ShapeDtypeStruct(xs.shape, F32),
        compiler_params=pltpu.CompilerParams(
            dimension_semantics=("arbitrary",), vmem_limit_bytes=VMEM_LIMIT),
        name="experts",
    )(*visits, xs, w_e_in, b_e_in.reshape(N_EXPERTS, 1, 2 * D_FF), w_e_out,
      b_e_out.reshape(N_EXPERTS, 1, D_MODEL))


def _visit_schedule(counts, n_rows):
    n_tiles = n_rows // EXPERT_ROWS
    max_visits = n_tiles + N_EXPERTS - 1
    ends = jnp.cumsum(counts)
    starts = ends - counts
    first_tile = starts // EXPERT_ROWS
    tiles_e = jnp.where(counts > 0, (ends - 1) // EXPERT_ROWS - first_tile + 1, 0)
    vis_end = jnp.cumsum(tiles_e)
    vis_start = vis_end - tiles_e
    total = vis_end[-1]
    v = jnp.minimum(jnp.arange(max_visits, dtype=jnp.int32), total - 1)
    e = jnp.sum((vis_end[None, :] <= v[:, None]).astype(jnp.int32), axis=1)
    is_e = e[:, None] == jnp.arange(N_EXPERTS, dtype=jnp.int32)[None, :]
    pick = lambda table: jnp.sum(jnp.where(is_e, table[None, :], 0), axis=1)
    tile = pick(first_tile) + (v - pick(vis_start))
    lo = jnp.maximum(pick(starts), tile * EXPERT_ROWS) - tile * EXPERT_ROWS
    hi = jnp.minimum(pick(ends), (tile + 1) * EXPERT_ROWS) - tile * EXPERT_ROWS
    prev_tile = jnp.concatenate([jnp.full((1,), -1, jnp.int32), tile[:-1]])
    prev_e = jnp.concatenate([jnp.full((1,), -1, jnp.int32), e[:-1]])
    experts = jnp.arange(N_EXPERTS, dtype=jnp.int32)
    later = (experts[None, :] > experts[:, None]) & (tiles_e[None, :] > 0)
    next_e = jnp.min(jnp.where(later, experts[None, :], N_EXPERTS), axis=1)
    next_e = jnp.where(next_e == N_EXPERTS, -1, next_e)
    as_i32 = lambda a: a.astype(jnp.int32)
    return (as_i32(tile), as_i32(e), as_i32(lo), as_i32(hi), as_i32(tile != prev_tile),
            as_i32(e != prev_e), as_i32(pick(next_e)), as_i32(total).reshape(1))


def _combine_kernel(tokens, pos_ref, x1_ref, meta_ref, p_ref, ys_hbm,
                    g_ple_ref, w_pg_ref, w_ple_ref, g_final_ref, y_ref, rows_even, rows_odd, sem_rows):
    i, n = pl.program_id(0), pl.num_programs(0)

    def copy(block, t, k, rows, sem):
        src_row = pos_ref[(block * tokens + t) * TOP_K + k]
        return _row_copy(ys_hbm, src_row, rows.at[k], t, sem)

    def wait_block(rows, sem):
        _for_each_row(tokens, lambda t, k: _row_copy(ys_hbm, 0, rows.at[0], 0, sem).wait())

    @pl.when(i == 0)
    def _():
        _for_each_row(tokens, lambda t, k: copy(0, t, k, rows_even, sem_rows.at[0]).start(priority=k % 2))

    def step(rows, sem, next_rows, next_sem):
        wait_block(rows, sem)
        next_block = lax.rem(i + 1, n)
        for t in range(tokens):
            for k in range(TOP_K):
                copy(next_block, t, k, next_rows, next_sem).start(priority=k % 2)

        meta = meta_ref[...]
        x1 = x1_ref[...]
        gates = [jnp.broadcast_to(meta[:, META_GATE + k:META_GATE + k + 1], (tokens, LANES))
                 for k in range(TOP_K)]
        cols = []
        for j, c, xcols in _token_chunks():
            acc = x1[:, xcols]
            for k in range(TOP_K):
                acc = acc + gates[k] * rows[k, j, pl.ds(c, tokens, stride=TOKEN_ROWS), :]
            cols.append(acc)
        x2 = jnp.concatenate(cols, axis=1)
        hp = _rms(x2, g_ple_ref[...]).astype(BF16)
        gate = jax.nn.sigmoid(_dot(hp, w_pg_ref[...]))
        x3 = x2 + _dot(p_ref[...].astype(BF16), w_ple_ref[...]) * gate
        y_ref[...] = _rms(x3, g_final_ref[...])

        @pl.when(i == n - 1)
        def _():
            wait_block(next_rows, next_sem)

    @pl.when(i % 2 == 0)
    def _():
        step(rows_even, sem_rows.at[0], rows_odd, sem_rows.at[1])

    @pl.when(i % 2 == 1)
    def _():
        step(rows_odd, sem_rows.at[1], rows_even, sem_rows.at[0])


def _combine(tokens, pos, x1, meta, p, ys, wts):
    n_out = x1.shape[0]
    tok = lambda w: pl.BlockSpec((tokens, w), lambda i, *_: (i, 0))
    grid_spec = pltpu.PrefetchScalarGridSpec(
        num_scalar_prefetch=1,
        grid=(n_out // tokens,),
        in_specs=[tok(D_MODEL), tok(LANES), tok(PLE_DIM), pl.BlockSpec(memory_space=pl.ANY)]
                 + [pl.BlockSpec(w.shape, lambda i, *_: (0, 0)) for w in wts],
        out_specs=tok(D_MODEL),
        scratch_shapes=[pltpu.VMEM((TOP_K,) + _token_shape(tokens), F32),
                        pltpu.VMEM((TOP_K,) + _token_shape(tokens), F32),
                        pltpu.SemaphoreType.DMA((2,))],
    )
    return pl.pallas_call(
        functools.partial(_combine_kernel, tokens),
        grid_spec=grid_spec,
        out_shape=jax.ShapeDtypeStruct((n_out, D_MODEL), F32),
        compiler_params=pltpu.CompilerParams(
            dimension_semantics=("arbitrary",), vmem_limit_bytes=VMEM_LIMIT),
        name="combine_ple",
    )(pos, x1, meta, p, ys, *wts)


def _rope_tables(pos, n_rows):
    out = []
    for dim in (ATT_HEAD_DIM, RET_DK):
        half = dim // 2
        inv = ROPE_THETA ** (-np.arange(half, dtype=np.float64) / half)
        ang = np.asarray(pos, np.float64)[:, None] * inv[None, :]
        cos, sin = np.cos(ang), np.sin(ang)
        reps = LANES // dim
        cos_row = np.tile(np.concatenate([cos, cos], axis=-1), (1, reps))
        sin_row = np.tile(np.concatenate([-sin, sin], axis=-1), (1, reps))
        out += [jnp.asarray(np.broadcast_to(cos_row, (n_rows, LANES)), F32),
                jnp.asarray(np.broadcast_to(sin_row, (n_rows, LANES)), F32)]
    return out


def _retention_tables():
    log_g = np.log(1.0 - 2.0 ** (-5.0 - np.arange(RET_HEADS, dtype=np.float64)))
    t = np.arange(BLK, dtype=np.float64)
    rel = t[:, None] - t[None, :]
    intra = np.where(rel >= 0, np.exp(np.maximum(rel, 0.0)[None] * log_g[:, None, None]), 0.0)
    cross = np.exp((t + 1.0)[None, :] * log_g[:, None])
    kdec = np.exp((BLK - 1.0 - t)[None, :] * log_g[:, None])
    lane_bcast = lambda a: np.broadcast_to(a[:, :, None], (RET_HEADS, BLK, LANES))
    tabs = tuple(jnp.asarray(a, F32) for a in (intra, lane_bcast(cross), lane_bcast(kdec)))
    return log_g, tabs


def kernel(x_prompt, x_sample, state_swa_k, state_swa_v, state_ret, p_prompt, p_sample, g_mix, w_in, b_in,
           attn_sinks, g_ret_norm, w_out, b_out, g_mlp, w_router, b_router, w_e_in, b_e_in, w_e_out,
           b_e_out, g_ple, w_ple_gate, w_ple, g_final):
    bsz, seq, _ = x_prompt.shape
    n_dec = x_sample.shape[0]
    n_prompt = bsz * seq
    assert x_sample.shape[1] == 1 and g_mix.shape[0] == 1
    assert state_swa_k.shape[2] == WINDOW and n_dec <= DISPATCH_TOKENS

    row = lambda a: a.reshape(1, -1)
    log_g, ret_tabs = _retention_tables()
    sinks = attn_sinks[0].astype(F32)
    scal_prompt = jnp.concatenate([sinks, jnp.asarray(np.exp(BLK * log_g), F32)])
    scal_sample = jnp.concatenate([sinks, jnp.asarray(np.exp(log_g), F32)])

    w_in_bf = w_in[0].astype(BF16)
    w_out_bf = w_out[0].astype(BF16)
    wr = jnp.pad(w_router[0], ((0, 0), (0, LANES - N_EXPERTS)))
    wr_hi = wr.astype(BF16)
    wr_lo = (wr - wr_hi.astype(F32)).astype(BF16)
    br = jnp.pad(b_router[0], (0, LANES - N_EXPERTS)).reshape(1, LANES)
    post_wts = (row(g_ret_norm[0]), w_out_bf, row(b_out[0]), row(g_mlp[0]), wr_hi, wr_lo, br)

    tabs_p = _rope_tables(np.arange(seq), seq)
    x1_p, h2_p, meta_p, cnt, kp, vp, st_p = _mixer_prompt(
        scal_prompt, x_prompt, tabs_p, ret_tabs, (row(g_mix[0]), w_in_bf, row(b_in[0])) + post_wts)

    tabs_s = _rope_tables(np.full((1,), PAST_LEN), n_dec)
    xs2 = x_sample.reshape(n_dec, D_MODEL)
    aq, ak, av, rq, rk, rv, rg = _sample_inproj(xs2, tabs_s, row(g_mix[0]), w_in_bf, row(b_in[0]))
    att, ret, ks, vs, st_s = _sample_state(
        scal_sample, aq, ak, av, rq, rk, rv,
        state_swa_k[0].reshape(n_dec, WINDOW, LANES), state_swa_v[0].reshape(n_dec, WINDOW, LANES),
        state_ret[0])
    x1_s, h2_s, meta_s, cnt = _sample_post(xs2, att, ret, rg, cnt, post_wts)

    counts = cnt[:, 0].astype(jnp.int32)
    starts = jnp.cumsum(counts) - counts
    def sorted_rows(meta):
        sel = meta[:, META_IDX:META_IDX + TOP_K].astype(jnp.int32)
        rank = meta[:, META_RANK:META_RANK + TOP_K].astype(jnp.int32)
        is_e = sel[:, :, None] == jnp.arange(N_EXPERTS, dtype=jnp.int32)[None, None, :]
        return (jnp.sum(jnp.where(is_e, starts[None, None, :], 0), axis=-1) + rank).reshape(-1)

    pos_p, pos_s = sorted_rows(meta_p), sorted_rows(meta_s)

    xs_sorted = _dispatch(pos_p, pos_s, h2_p, h2_s)
    visits = _visit_schedule(counts, (n_prompt + n_dec) * TOP_K)
    ys_sorted = _experts(visits, xs_sorted, w_e_in[0], b_e_in[0], w_e_out[0], b_e_out[0])

    ple_wts = (row(g_ple[0]), w_ple_gate[0].astype(BF16), w_ple[0].astype(BF16), row(g_final))
    y_p = _combine(COMBINE_TOKENS, pos_p, x1_p, meta_p, p_prompt[0].reshape(n_prompt, PLE_DIM),
                   ys_sorted, ple_wts)
    y_s = _combine(n_dec, pos_s, x1_s, meta_s, p_sample[0].reshape(n_dec, PLE_DIM), ys_sorted, ple_wts)

    kv_shape = (1, bsz, WINDOW, ATT_KV_HEADS, ATT_HEAD_DIM)
    dec_shape = (1, n_dec, WINDOW, ATT_KV_HEADS, ATT_HEAD_DIM)
    return (y_p.reshape(bsz, seq, D_MODEL), y_s.reshape(n_dec, 1, D_MODEL),
            kp.reshape(kv_shape), vp.reshape(kv_shape), st_p[None],
            ks.reshape(dec_shape), vs.reshape(dec_shape), st_s[None])
```

```python
import functools

import jax
import jax.numpy as jnp
import numpy as np
from jax import lax
from jax.experimental import pallas as pl
from jax.experimental.pallas import tpu as pltpu

D_MODEL = 1024
PAST_LEN = 16384
PLE_DIM = 256
ATT_HEADS = 8
ATT_KV_HEADS = 2
ATT_HEAD_DIM = 64
ATT_GROUP = ATT_HEADS // ATT_KV_HEADS
WINDOW = 128
BLK = 128
RET_HEADS = 4
RET_DK = 128
RET_DV = 128
ROPE_THETA = 10000.0
N_EXPERTS = 32
TOP_K = 4
D_FF = 1024
SWIGLU_LIMIT = 7.0
SWIGLU_ALPHA = 1.702
NORM_EPS = 1e-5
GN_EPS = 1e-6
NEG_INF = -1e30

ATT_Q_W = ATT_HEADS * ATT_HEAD_DIM
ATT_KV_W = ATT_KV_HEADS * ATT_HEAD_DIM
RET_W = RET_HEADS * RET_DK
IN_W = ATT_Q_W + 2 * ATT_KV_W + 4 * RET_W
O_AQ = 0
O_AK = ATT_Q_W
O_AV = O_AK + ATT_KV_W
O_RQ = O_AV + ATT_KV_W
O_RK = O_RQ + RET_W
O_RV = O_RK + RET_W
O_RG = O_RV + RET_W

LANES = 128
META_IDX, META_RANK, META_GATE = 0, 4, 8

MIX_TOKENS = 1024
DISPATCH_TOKENS = 256
EXPERT_ROWS = 256
COMBINE_TOKENS = 256
SAMPLE_CHUNK = 16
ROW_UNROLL = 4
X_RING = 3
VMEM_LIMIT = 56 * 1024 * 1024

F32 = jnp.float32
BF16 = jnp.bfloat16


def _dot(a, b):
    return jnp.dot(a, b, preferred_element_type=F32)


def _dot_nt(a, b):
    return lax.dot_general(a, b, (((1,), (1,)), ((), ())), preferred_element_type=F32)


def _dot_tn(a, b):
    return lax.dot_general(a, b, (((0,), (0,)), ((), ())), preferred_element_type=F32)


TOKEN_SLABS = 2
TOKEN_ROWS = D_MODEL // (TOKEN_SLABS * LANES)


def _token_shape(n_tokens):
    return (TOKEN_SLABS, n_tokens * TOKEN_ROWS, LANES)


def _token_chunks():
    for c in range(TOKEN_ROWS):
        for j in range(TOKEN_SLABS):
            yield j, c, slice((TOKEN_SLABS * c + j) * LANES, (TOKEN_SLABS * c + j + 1) * LANES)


def _store_token_tiles(ref, first_token, rows):
    n = rows.shape[0]
    for j, c, cols in _token_chunks():
        ref[j, pl.ds(first_token * TOKEN_ROWS + c, n, stride=TOKEN_ROWS), :] = rows[:, cols]


def _load_token_tiles(ref, n):
    return jnp.concatenate(
        [ref[j, pl.ds(c, n, stride=TOKEN_ROWS), :] for j, c, _ in _token_chunks()], axis=1)


def _rms(x, g):
    return x * lax.rsqrt(jnp.mean(x * x, axis=-1, keepdims=True) + NORM_EPS) * g


def _rope(col, cos, sin_signed, half):
    lane = lax.broadcasted_iota(jnp.int32, col.shape, 1)
    fwd = pltpu.roll(col, LANES - half, 1)
    bwd = pltpu.roll(col, half, 1)
    swapped = jnp.where((lane % (2 * half)) < half, fwd, bwd)
    return col * cos + swapped * sin_signed


def _in_proj(x, g_mix, w_in, b_in, cos64, sin64, cos128, sin128):
    h = _rms(x, g_mix).astype(BF16)
    z = _dot(h, w_in) + b_in

    def cols(off, width):
        return [z[:, off + c * LANES: off + (c + 1) * LANES] for c in range(width // LANES)]

    scale_q = ATT_HEAD_DIM ** -0.5
    aq = [_rope(c, cos64, sin64, ATT_HEAD_DIM // 2) * scale_q for c in cols(O_AQ, ATT_Q_W)]
    ak = _rope(z[:, O_AK:O_AK + LANES], cos64, sin64, ATT_HEAD_DIM // 2)
    av = z[:, O_AV:O_AV + LANES]
    rq = [_rope(c, cos128, sin128, RET_DK // 2) for c in cols(O_RQ, RET_W)]
    rk = [_rope(c, cos128, sin128, RET_DK // 2) * (RET_DK ** -0.5) for c in cols(O_RK, RET_W)]
    rv = cols(O_RV, RET_W)
    rg = z[:, O_RG:O_RG + RET_W]
    return aq, ak, av, rq, rk, rv, rg


def _expand_q(aq_cols, rows):
    lane = lax.broadcasted_iota(jnp.int32, (rows, LANES), 1)
    low = lane < ATT_HEAD_DIM
    pieces = []
    for h in range(ATT_HEADS):
        col, half, kv = aq_cols[h // 2], h % 2, h // ATT_GROUP
        if half != kv:
            col = pltpu.roll(col, ATT_HEAD_DIM, 1)
        pieces.append(jnp.where(low if kv == 0 else jnp.logical_not(low), col, 0.0))
    return jnp.concatenate(pieces, axis=0)


def _collect_heads(o, rows):
    lane = lax.broadcasted_iota(jnp.int32, (rows, LANES), 1)
    low = lane < ATT_HEAD_DIM
    out = []
    for c in range(ATT_HEADS // 2):
        pieces = []
        for half in (0, 1):
            h = 2 * c + half
            oh = o[h * rows:(h + 1) * rows]
            if h // ATT_GROUP != half:
                oh = pltpu.roll(oh, ATT_HEAD_DIM, 1)
            pieces.append(oh)
        out.append(jnp.where(low, pieces[0], pieces[1]))
    return jnp.concatenate(out, axis=1)


def _sink_softmax(s, mask, sink):
    s = jnp.where(mask, s, NEG_INF)
    m = jnp.maximum(jnp.max(s, axis=-1, keepdims=True), sink)
    e = jnp.exp(s - m)
    den = jnp.sum(e, axis=-1, keepdims=True) + jnp.exp(sink - m)
    return e * (1.0 / den)


def _group_norm_gate(ret, rg, g_ret):
    cols = []
    for h in range(RET_HEADS):
        of = ret[:, h * RET_DV:(h + 1) * RET_DV]
        d = of - jnp.mean(of, axis=-1, keepdims=True)
        var = jnp.mean(d * d, axis=-1, keepdims=True)
        cols.append(d * lax.rsqrt(var + GN_EPS))
    y = jnp.concatenate(cols, axis=1) * g_ret
    return y * (rg * jax.nn.sigmoid(rg))


def _route(x1, g_mlp, wr_hi, wr_lo, b_router, cnt):
    n = x1.shape[0]
    h2 = _rms(x1, g_mlp)
    hi = h2.astype(BF16)
    lo = (h2 - hi.astype(F32)).astype(BF16)
    logits = _dot(hi, wr_hi) + (_dot(hi, wr_lo) + _dot(lo, wr_hi)) + b_router
    work = logits.T[:N_EXPERTS]
    expert = lax.broadcasted_iota(jnp.int32, (N_EXPERTS, n), 0)
    idxs, vals = [], []
    for _ in range(TOP_K):
        m = jnp.max(work, axis=0, keepdims=True)
        idx = jnp.min(jnp.where(work == m, expert, N_EXPERTS), axis=0, keepdims=True)
        idxs.append(idx)
        vals.append(m)
        work = jnp.where(expert == idx, -jnp.inf, work)
    es = [jnp.exp(v - vals[0]) for v in vals]
    den = es[0] + es[1] + es[2] + es[3]
    gates = [e / den for e in es]

    sel = (expert == idxs[0]) | (expert == idxs[1]) | (expert == idxs[2]) | (expert == idxs[3])
    onehot = jnp.where(sel, 1.0, 0.0)
    r = lax.broadcasted_iota(jnp.int32, (n, n), 0)
    c = lax.broadcasted_iota(jnp.int32, (n, n), 1)
    earlier = jnp.where(r < c, 1.0, 0.0).astype(BF16)
    before = _dot(onehot.astype(BF16), earlier) + cnt
    row = lax.broadcasted_iota(jnp.int32, (LANES, n), 0)
    meta_t = jnp.zeros((LANES, n), F32)
    for k in range(TOP_K):
        rank = jnp.sum(jnp.where(expert == idxs[k], before, 0.0), axis=0, keepdims=True)
        meta_t = jnp.where(row == META_IDX + k, idxs[k].astype(F32), meta_t)
        meta_t = jnp.where(row == META_RANK + k, rank, meta_t)
        meta_t = jnp.where(row == META_GATE + k, gates[k], meta_t)
    return h2, meta_t.T, cnt + jnp.sum(onehot, axis=1, keepdims=True)


def _mixer_prompt_kernel(scal_ref, x_ref, cos64_ref, sin64_ref, cos128_ref, sin128_ref,
                         intra_ref, cross_ref, kdec_ref,
                         g_mix_ref, w_in_ref, b_in_ref, g_ret_ref, w_out_ref, b_out_ref,
                         g_mlp_ref, wr_hi_ref, wr_lo_ref, b_router_ref,
                         x1_ref, h2_ref, meta_ref, cnt_out_ref, kp_ref, vp_ref, stp_ref,
                         prevk, prevv, st, cnt):
    b, n = pl.program_id(0), pl.program_id(1)

    @pl.when((b == 0) & (n == 0))
    def _():
        cnt[...] = jnp.zeros_like(cnt)

    @pl.when(n == 0)
    def _():
        prevk[...] = jnp.zeros_like(prevk)
        prevv[...] = jnp.zeros_like(prevv)
        st[...] = jnp.zeros_like(st)

    x = x_ref[...]
    aq, ak, av, rq, rk, rv, rg = _in_proj(
        x, g_mix_ref[...], w_in_ref[...], b_in_ref[...],
        cos64_ref[...], sin64_ref[...], cos128_ref[...], sin128_ref[...])

    row = lax.broadcasted_iota(jnp.int32, (BLK, 2 * BLK), 0)
    col = lax.broadcasted_iota(jnp.int32, (BLK, 2 * BLK), 1)
    mask_prev = (col < BLK) & (col > row)
    mask_cur = (col >= BLK) & (col - BLK <= row)

    k_prev, v_prev = prevk[...], prevv[...]
    mixed = []
    for j in range(MIX_TOKENS // BLK):
        sl = slice(j * BLK, (j + 1) * BLK)
        kk = jnp.concatenate([k_prev, ak[sl]], axis=0).astype(BF16)
        vv = jnp.concatenate([v_prev, av[sl]], axis=0).astype(BF16)
        qall = _expand_q([cq[sl] for cq in aq], BLK).astype(BF16)
        s = _dot_nt(qall, kk)
        mask = mask_cur | mask_prev if j > 0 else mask_cur | (mask_prev & (n > 0))
        probs = [_sink_softmax(s[h * BLK:(h + 1) * BLK], mask, scal_ref[h]) for h in range(ATT_HEADS)]
        o = _dot(jnp.concatenate(probs, axis=0).astype(BF16), vv)
        att = _collect_heads(o, BLK)
        k_prev, v_prev = ak[sl], av[sl]

        ret = []
        for h in range(RET_HEADS):
            qb, kh, vb = rq[h][sl].astype(BF16), rk[h][sl], rv[h][sl].astype(BF16)
            sc = _dot_nt(qb, kh.astype(BF16)) * intra_ref[h]
            sth = st[h]
            o_h = _dot(sc.astype(BF16), vb) + _dot(qb, sth.astype(BF16)) * cross_ref[h]
            st[h] = sth * scal_ref[ATT_HEADS + h] + _dot_tn((kh * kdec_ref[h]).astype(BF16), vb)
            ret.append(o_h)
        ret_out = _group_norm_gate(jnp.concatenate(ret, axis=1), rg[sl], g_ret_ref[...])
        mixed.append(jnp.concatenate([att, ret_out], axis=1))

    mixed = jnp.concatenate(mixed, axis=0).astype(BF16)
    x1 = x + _dot(mixed, w_out_ref[...]) + b_out_ref[...]
    x1_ref[...] = x1

    c = cnt[...]
    for j in range(MIX_TOKENS // BLK):
        sl = slice(j * BLK, (j + 1) * BLK)
        h2, meta, c = _route(x1[sl], g_mlp_ref[...], wr_hi_ref[...], wr_lo_ref[...], b_router_ref[...], c)
        _store_token_tiles(h2_ref, j * BLK, h2)
        meta_ref[sl, :] = meta
    cnt[...] = c
    cnt_out_ref[...] = c

    prevk[...] = k_prev
    prevv[...] = v_prev
    kp_ref[0] = k_prev
    vp_ref[0] = v_prev
    stp_ref[0] = st[...]


def _const_spec(shape):
    nd = len(shape)
    return pl.BlockSpec(shape, lambda *_: (0,) * nd)


def _mixer_prompt(scal, x, tabs, ret_tabs, wts):
    bsz, seq, _ = x.shape
    steps = seq // MIX_TOKENS
    x2 = x.reshape(bsz * seq, D_MODEL)
    tok = lambda w: pl.BlockSpec((MIX_TOKENS, w), lambda b, n: (b * steps + n, 0))
    pos = lambda w: pl.BlockSpec((MIX_TOKENS, w), lambda b, n: (n, 0))
    in_specs = ([pl.BlockSpec(memory_space=pltpu.SMEM), tok(D_MODEL)]
                + [pos(LANES)] * 4
                + [_const_spec(t.shape) for t in ret_tabs]
                + [_const_spec(w.shape) for w in wts])
    out_shape = (
        jax.ShapeDtypeStruct((bsz * seq, D_MODEL), F32),
        jax.ShapeDtypeStruct(_token_shape(bsz * seq), F32),
        jax.ShapeDtypeStruct((bsz * seq, LANES), F32),
        jax.ShapeDtypeStruct((N_EXPERTS, LANES), F32),
        jax.ShapeDtypeStruct((bsz, BLK, LANES), F32),
        jax.ShapeDtypeStruct((bsz, BLK, LANES), F32),
        jax.ShapeDtypeStruct((bsz, RET_HEADS, RET_DK, RET_DV), F32),
    )
    out_specs = (
        tok(D_MODEL), pl.BlockSpec(_token_shape(MIX_TOKENS), lambda b, n: (0, b * steps + n, 0)),
        tok(LANES), _const_spec((N_EXPERTS, LANES)),
        pl.BlockSpec((1, BLK, LANES), lambda b, n: (b, 0, 0)),
        pl.BlockSpec((1, BLK, LANES), lambda b, n: (b, 0, 0)),
        pl.BlockSpec((1, RET_HEADS, RET_DK, RET_DV), lambda b, n: (b, 0, 0, 0)),
    )
    return pl.pallas_call(
        _mixer_prompt_kernel,
        grid=(bsz, steps),
        in_specs=in_specs,
        out_specs=out_specs,
        out_shape=out_shape,
        scratch_shapes=[pltpu.VMEM((BLK, LANES), F32), pltpu.VMEM((BLK, LANES), F32),
                        pltpu.VMEM((RET_HEADS, RET_DK, RET_DV), F32), pltpu.VMEM((N_EXPERTS, LANES), F32)],
        compiler_params=pltpu.CompilerParams(
            dimension_semantics=("arbitrary", "arbitrary"), vmem_limit_bytes=VMEM_LIMIT),
        name="mixer_prompt",
    )(scal, x2, *tabs, *ret_tabs, *wts)


def _sample_inproj_kernel(x_ref, cos64_ref, sin64_ref, cos128_ref, sin128_ref,
                          g_mix_ref, w_in_ref, b_in_ref,
                          aq_ref, ak_ref, av_ref, rq_ref, rk_ref, rv_ref, rg_ref):
    aq, ak, av, rq, rk, rv, rg = _in_proj(
        x_ref[...], g_mix_ref[...], w_in_ref[...], b_in_ref[...],
        cos64_ref[...], sin64_ref[...], cos128_ref[...], sin128_ref[...])
    aq_ref[...] = jnp.concatenate(aq, axis=1)
    ak_ref[...] = ak
    av_ref[...] = av
    rq_ref[...] = jnp.concatenate(rq, axis=1)
    rk_ref[...] = jnp.concatenate(rk, axis=1)
    rv_ref[...] = jnp.concatenate(rv, axis=1)
    rg_ref[...] = rg


def _sample_inproj(x, tabs, g_mix, w_in, b_in):
    n = x.shape[0]
    widths = (ATT_Q_W, LANES, LANES, RET_W, RET_W, RET_W, RET_W)
    args = (x, *tabs, g_mix, w_in, b_in)
    return pl.pallas_call(
        _sample_inproj_kernel,
        grid=(1,),
        in_specs=[_const_spec(a.shape) for a in args],
        out_specs=tuple(_const_spec((n, w)) for w in widths),
        out_shape=tuple(jax.ShapeDtypeStruct((n, w), F32) for w in widths),
        compiler_params=pltpu.CompilerParams(vmem_limit_bytes=VMEM_LIMIT),
        name="sample_inproj",
    )(*args)


def _sample_state_kernel(scal_ref, aq_ref, ak_ref, av_ref, rq_ref, rk_ref, rv_ref,
                         kbuf_ref, vbuf_ref, st_ref,
                         att_ref, ret_ref, kout_ref, vout_ref, stout_ref):
    nb = SAMPLE_CHUNK
    rows = nb * WINDOW

    def shift_in(buf_ref, new):
        flat = buf_ref[...].reshape(rows, LANES)
        rolled = pltpu.roll(flat, rows - 1, 0)
        rolled = rolled.reshape(nb, WINDOW, LANES)
        pos = lax.broadcasted_iota(jnp.int32, (nb, WINDOW, LANES), 1)
        return jnp.where(pos == WINDOW - 1, new[:, None, :], rolled)

    knew = shift_in(kbuf_ref, ak_ref[...])
    vnew = shift_in(vbuf_ref, av_ref[...])
    kout_ref[...] = knew
    vout_ref[...] = vnew

    aq = aq_ref[...]
    qall = _expand_q([aq[:, c * LANES:(c + 1) * LANES] for c in range(ATT_Q_W // LANES)], nb)
    s = _dot_nt(qall.astype(BF16), knew.reshape(rows, LANES).astype(BF16))
    r = lax.broadcasted_iota(jnp.int32, (nb, rows), 0)
    c = lax.broadcasted_iota(jnp.int32, (nb, rows), 1)
    own = (c // WINDOW) == r
    probs = [_sink_softmax(s[h * nb:(h + 1) * nb], own, scal_ref[h]) for h in range(ATT_HEADS)]
    o = _dot(jnp.concatenate(probs, axis=0).astype(BF16), vnew.reshape(rows, LANES).astype(BF16))
    att_ref[...] = _collect_heads(o, nb)

    qt = rq_ref[...].T
    kt = rk_ref[...].T
    rv = rv_ref[...]
    for bi in range(nb):
        for h in range(RET_HEADS):
            hs = slice(h * RET_DK, (h + 1) * RET_DK)
            s_new = st_ref[bi, h] * scal_ref[ATT_HEADS + h] + kt[hs, bi:bi + 1] * rv[bi:bi + 1, hs]
            stout_ref[bi, h] = s_new
            ret_ref[bi:bi + 1, hs] = jnp.sum(qt[hs, bi:bi + 1] * s_new, axis=0, keepdims=True)


def _sample_state(scal, aq, ak, av, rq, rk, rv, kbuf, vbuf, st):
    n = aq.shape[0]
    nb = SAMPLE_CHUNK
    row = lambda w: pl.BlockSpec((nb, w), lambda i: (i, 0))
    buf = pl.BlockSpec((nb, WINDOW, LANES), lambda i: (i, 0, 0))
    stt = pl.BlockSpec((nb, RET_HEADS, RET_DK, RET_DV), lambda i: (i, 0, 0, 0))
    return pl.pallas_call(
        _sample_state_kernel,
        grid=(n // nb,),
        in_specs=[pl.BlockSpec(memory_space=pltpu.SMEM), row(ATT_Q_W), row(LANES), row(LANES),
                  row(RET_W), row(RET_W), row(RET_W), buf, buf, stt],
        out_specs=(row(ATT_Q_W), row(RET_W), buf, buf, stt),
        out_shape=(jax.ShapeDtypeStruct((n, ATT_Q_W), F32), jax.ShapeDtypeStruct((n, RET_W), F32),
                   jax.ShapeDtypeStruct(kbuf.shape, F32), jax.ShapeDtypeStruct(vbuf.shape, F32),
                   jax.ShapeDtypeStruct(st.shape, F32)),
        compiler_params=pltpu.CompilerParams(
            dimension_semantics=("arbitrary",), vmem_limit_bytes=VMEM_LIMIT),
        name="sample_state",
    )(scal, aq, ak, av, rq, rk, rv, kbuf, vbuf, st)


def _sample_post_kernel(x_ref, att_ref, ret_ref, rg_ref, cnt_in_ref,
                        g_ret_ref, w_out_ref, b_out_ref, g_mlp_ref, wr_hi_ref, wr_lo_ref, b_router_ref,
                        x1_ref, h2_ref, meta_ref, cnt_out_ref):
    x = x_ref[...]
    ret_out = _group_norm_gate(ret_ref[...], rg_ref[...], g_ret_ref[...])
    mixed = jnp.concatenate([att_ref[...], ret_out], axis=1).astype(BF16)
    x1 = x + _dot(mixed, w_out_ref[...]) + b_out_ref[...]
    x1_ref[...] = x1
    h2, meta, c = _route(x1, g_mlp_ref[...], wr_hi_ref[...], wr_lo_ref[...], b_router_ref[...],
                         cnt_in_ref[...])
    _store_token_tiles(h2_ref, 0, h2)
    meta_ref[...] = meta
    cnt_out_ref[...] = c


def _sample_post(x, att, ret, rg, cnt, wts):
    n = x.shape[0]
    vm = [x, att, ret, rg, cnt, *wts]
    shapes = ((n, D_MODEL), _token_shape(n), (n, LANES), (N_EXPERTS, LANES))
    return pl.pallas_call(
        _sample_post_kernel,
        grid=(1,),
        in_specs=[_const_spec(a.shape) for a in vm],
        out_specs=tuple(_const_spec(s) for s in shapes),
        out_shape=tuple(jax.ShapeDtypeStruct(s, F32) for s in shapes),
        compiler_params=pltpu.CompilerParams(vmem_limit_bytes=VMEM_LIMIT),
        name="sample_post",
    )(*vm)


def _row_copy(src, src_token, dst, dst_token, sem):
    rows = lambda tok: pl.ds(pl.multiple_of(tok * TOKEN_ROWS, TOKEN_ROWS), TOKEN_ROWS)
    return pltpu.make_async_copy(src.at[:, rows(src_token)], dst.at[:, rows(dst_token)], sem)


def _for_each_row(n_tok, fn):
    def body(g, carry):
        for u in range(ROW_UNROLL):
            for k in range(TOP_K):
                fn(g * ROW_UNROLL + u, k)
        return carry

    lax.fori_loop(0, n_tok // ROW_UNROLL, body, 0)


def _dispatch_kernel(prompt_steps, n_dec, pos_p, pos_s, h2p_ref, h2s_ref, xs_hbm, sem_rows):
    i = pl.program_id(0)

    def scatter_rows(src_ref, n_tok, pos_ref, first_idx):
        def copy(t, k):
            return _row_copy(src_ref, t, xs_hbm, pos_ref[first_idx + t * TOP_K + k], sem_rows)

        _for_each_row(n_tok, lambda t, k: copy(t, k).start(priority=k % 2))
        _for_each_row(n_tok, lambda t, k: _row_copy(src_ref, 0, xs_hbm, 0, sem_rows).wait())

    @pl.when(i < prompt_steps)
    def _():
        scatter_rows(h2p_ref, DISPATCH_TOKENS, pos_p, i * (DISPATCH_TOKENS * TOP_K))

    @pl.when(i == prompt_steps)
    def _():
        scatter_rows(h2s_ref, n_dec, pos_s, 0)


def _dispatch(pos_p, pos_s, h2_p, h2_s):
    n_prompt, n_dec = h2_p.shape[1] // TOKEN_ROWS, h2_s.shape[1] // TOKEN_ROWS
    prompt_steps = n_prompt // DISPATCH_TOKENS
    grid_spec = pltpu.PrefetchScalarGridSpec(
        num_scalar_prefetch=2,
        grid=(prompt_steps + 1,),
        in_specs=[pl.BlockSpec(_token_shape(DISPATCH_TOKENS),
                               lambda i, *_: (0, jnp.minimum(i, prompt_steps - 1), 0)),
                  pl.BlockSpec(h2_s.shape, lambda i, *_: (0, 0, 0))],
        out_specs=pl.BlockSpec(memory_space=pl.ANY),
        scratch_shapes=[pltpu.SemaphoreType.DMA],
    )
    return pl.pallas_call(
        functools.partial(_dispatch_kernel, prompt_steps, n_dec),
        grid_spec=grid_spec,
        out_shape=jax.ShapeDtypeStruct(_token_shape((n_prompt + n_dec) * TOP_K), F32),
        compiler_params=pltpu.CompilerParams(
            dimension_semantics=("arbitrary",), vmem_limit_bytes=VMEM_LIMIT),
        name="dispatch",
    )(pos_p, pos_s, h2_p, h2_s)


def _experts_kernel(v_tile, v_expert, v_lo, v_hi, v_newtile, v_newexp, v_next, n_visits,
                    xs_hbm, w_in_hbm, b_in_ref, w_out_hbm, b_out_ref, y_ref,
                    w_in_f32, w_out_f32, w_in_bf, w_out_bf, sems, x_ring, sem_x):
    v = pl.program_id(0)
    total = n_visits[0]
    tile_rows = EXPERT_ROWS * TOKEN_ROWS

    def tile_copy(visit):
        rows = pl.ds(pl.multiple_of(v_tile[visit] * tile_rows, tile_rows), tile_rows)
        slot = visit % X_RING
        return pltpu.make_async_copy(xs_hbm.at[:, rows], x_ring.at[slot], sem_x.at[slot])

    @pl.when(v == 0)
    def _():
        for ahead in range(X_RING - 1):
            @pl.when(ahead < total)
            def _():
                tile_copy(ahead).start()

    def weight_copies(e):
        return (pltpu.make_async_copy(w_in_hbm.at[e], w_in_f32, sems.at[0]),
                pltpu.make_async_copy(w_out_hbm.at[e], w_out_f32, sems.at[1]))

    @pl.when(v < n_visits[0])
    def _():
        @pl.when(v_newexp[v] == 1)
        def _():
            @pl.when(v == 0)
            def _():
                for cp in weight_copies(v_expert[0]):
                    cp.start(priority=1)

            for cp in weight_copies(v_expert[v]):
                cp.wait()
            w_in_bf[...] = w_in_f32[...].astype(BF16)
            w_out_bf[...] = w_out_f32[...].astype(BF16)

            @pl.when(v_next[v] >= 0)
            def _():
                for cp in weight_copies(v_next[v]):
                    cp.start(priority=1)

        @pl.when(v + X_RING - 1 < total)
        def _():
            tile_copy(v + X_RING - 1).start()

        tile_copy(v).wait()
        x = _load_token_tiles(x_ring.at[v % X_RING], EXPERT_ROWS).astype(BF16)
        hg = _dot(x, w_in_bf[...]) + b_in_ref[0]
        gate = jnp.minimum(hg[:, :D_FF], SWIGLU_LIMIT)
        lin = jnp.clip(hg[:, D_FF:], -SWIGLU_LIMIT, SWIGLU_LIMIT)
        act = gate * jax.nn.sigmoid(SWIGLU_ALPHA * gate) * (lin + 1.0)
        y = _dot(act.astype(BF16), w_out_bf[...]) + b_out_ref[0]

        row = lax.broadcasted_iota(jnp.int32, (EXPERT_ROWS, LANES), 0)
        mine = (row >= v_lo[v]) & (row < v_hi[v])

        def write_rows(first_visit_of_tile):
            for j, c, cols in _token_chunks():
                rows_c = pl.ds(c, EXPERT_ROWS, stride=TOKEN_ROWS)
                keep = 0.0 if first_visit_of_tile else y_ref[j, rows_c, :]
                y_ref[j, rows_c, :] = jnp.where(mine, y[:, cols], keep)

        @pl.when(v_newtile[v] == 1)
        def _():
            write_rows(True)

        @pl.when(v_newtile[v] == 0)
        def _():
            write_rows(False)


def _experts(visits, xs, w_e_in, b_e_in, w_e_out, b_e_out):
    n_rows = xs.shape[1] // TOKEN_ROWS
    max_visits = n_rows // EXPERT_ROWS + N_EXPERTS - 1
    tile_rows = pl.BlockSpec(_token_shape(EXPERT_ROWS), lambda v, vt, *_: (0, vt[v], 0))
    any_spec = pl.BlockSpec(memory_space=pl.ANY)
    grid_spec = pltpu.PrefetchScalarGridSpec(
        num_scalar_prefetch=len(visits),
        grid=(max_visits,),
        in_specs=[
            any_spec, any_spec,
            pl.BlockSpec((1, 1, 2 * D_FF), lambda v, vt, ve, *_: (ve[v], 0, 0)),
            any_spec,
            pl.BlockSpec((1, 1, D_MODEL), lambda v, vt, ve, *_: (ve[v], 0, 0)),
        ],
        out_specs=tile_rows,
        scratch_shapes=[pltpu.VMEM((D_MODEL, 2 * D_FF), F32), pltpu.VMEM((D_FF, D_MODEL), F32),
                        pltpu.VMEM((D_MODEL, 2 * D_FF), BF16), pltpu.VMEM((D_FF, D_MODEL), BF16),
                        pltpu.SemaphoreType.DMA((2,)),
                        pltpu.VMEM((X_RING,) + _token_shape(EXPERT_ROWS), F32), pltpu.SemaphoreType.DMA((X_RING,))],
    )
    return pl.pallas_call(
        _experts_kernel,
        grid_spec=grid_spec,
        out_shape=jax.ShapeDtypeStruct(xs.shape, F32),
        compiler_params=pltpu.CompilerParams(
            dimension_semantics=("arbitrary",), vmem_limit_bytes=VMEM_LIMIT),
        name="experts",
    )(*visits, xs, w_e_in, b_e_in.reshape(N_EXPERTS, 1, 2 * D_FF), w_e_out,
      b_e_out.reshape(N_EXPERTS, 1, D_MODEL))


def _visit_schedule(counts, n_rows):
    n_tiles = n_rows // EXPERT_ROWS
    max_visits = n_tiles + N_EXPERTS - 1
    ends = jnp.cumsum(counts)
    starts = ends - counts
    first_tile = starts // EXPERT_ROWS
    tiles_e = jnp.where(counts > 0, (ends - 1) // EXPERT_ROWS - first_tile + 1, 0)
    vis_end = jnp.cumsum(tiles_e)
    vis_start = vis_end - tiles_e
    total = vis_end[-1]
    v = jnp.minimum(jnp.arange(max_visits, dtype=jnp.int32), total - 1)
    e = jnp.sum((vis_end[None, :] <= v[:, None]).astype(jnp.int32), axis=1)
    is_e = e[:, None] == jnp.arange(N_EXPERTS, dtype=jnp.int32)[None, :]
    pick = lambda table: jnp.sum(jnp.where(is_e, table[None, :], 0), axis=1)
    tile = pick(first_tile) + (v - pick(vis_start))
    lo = jnp.maximum(pick(starts), tile * EXPERT_ROWS) - tile * EXPERT_ROWS
    hi = jnp.minimum(pick(ends), (tile + 1) * EXPERT_ROWS) - tile * EXPERT_ROWS
    prev_tile = jnp.concatenate([jnp.full((1,), -1, jnp.int32), tile[:-1]])
    prev_e = jnp.concatenate([jnp.full((1,), -1, jnp.int32), e[:-1]])
    experts = jnp.arange(N_EXPERTS, dtype=jnp.int32)
    later = (experts[None, :] > experts[:, None]) & (tiles_e[None, :] > 0)
    next_e = jnp.min(jnp.where(later, experts[None, :], N_EXPERTS), axis=1)
    next_e = jnp.where(next_e == N_EXPERTS, -1, next_e)
    as_i32 = lambda a: a.astype(jnp.int32)
    return (as_i32(tile), as_i32(e), as_i32(lo), as_i32(hi), as_i32(tile != prev_tile),
            as_i32(e != prev_e), as_i32(pick(next_e)), as_i32(total).reshape(1))


def _combine_kernel(tokens, pos_ref, x1_ref, meta_ref, p_ref, ys_hbm,
                    g_ple_ref, w_pg_ref, w_ple_ref, g_final_ref, y_ref, rows_even, rows_odd, sem_rows):
    i, n = pl.program_id(0), pl.num_programs(0)

    def copy(block, t, k, rows, sem):
        src_row = pos_ref[(block * tokens + t) * TOP_K + k]
        return _row_copy(ys_hbm, src_row, rows.at[k], t, sem)

    def wait_block(rows, sem):
        _for_each_row(tokens, lambda t, k: _row_copy(ys_hbm, 0, rows.at[0], 0, sem).wait())

    @pl.when(i == 0)
    def _():
        _for_each_row(tokens, lambda t, k: copy(0, t, k, rows_even, sem_rows.at[0]).start(priority=k % 2))

    def step(rows, sem, next_rows, next_sem):
        wait_block(rows, sem)
        next_block = lax.rem(i + 1, n)
        for t in range(tokens):
            for k in range(TOP_K):
                copy(next_block, t, k, next_rows, next_sem).start(priority=k % 2)

        meta = meta_ref[...]
        x1 = x1_ref[...]
        gates = [jnp.broadcast_to(meta[:, META_GATE + k:META_GATE + k + 1], (tokens, LANES))
                 for k in range(TOP_K)]
        cols = []
        for j, c, xcols in _token_chunks():
            acc = x1[:, xcols]
            for k in range(TOP_K):
                acc = acc + gates[k] * rows[k, j, pl.ds(c, tokens, stride=TOKEN_ROWS), :]
            cols.append(acc)
        x2 = jnp.concatenate(cols, axis=1)
        hp = _rms(x2, g_ple_ref[...]).astype(BF16)
        gate = jax.nn.sigmoid(_dot(hp, w_pg_ref[...]))
        x3 = x2 + _dot(p_ref[...].astype(BF16), w_ple_ref[...]) * gate
        y_ref[...] = _rms(x3, g_final_ref[...])

        @pl.when(i == n - 1)
        def _():
            wait_block(next_rows, next_sem)

    @pl.when(i % 2 == 0)
    def _():
        step(rows_even, sem_rows.at[0], rows_odd, sem_rows.at[1])

    @pl.when(i % 2 == 1)
    def _():
        step(rows_odd, sem_rows.at[1], rows_even, sem_rows.at[0])


def _combine(tokens, pos, x1, meta, p, ys, wts):
    n_out = x1.shape[0]
    tok = lambda w: pl.BlockSpec((tokens, w), lambda i, *_: (i, 0))
    grid_spec = pltpu.PrefetchScalarGridSpec(
        num_scalar_prefetch=1,
        grid=(n_out // tokens,),
        in_specs=[tok(D_MODEL), tok(LANES), tok(PLE_DIM), pl.BlockSpec(memory_space=pl.ANY)]
                 + [pl.BlockSpec(w.shape, lambda i, *_: (0, 0)) for w in wts],
        out_specs=tok(D_MODEL),
        scratch_shapes=[pltpu.VMEM((TOP_K,) + _token_shape(tokens), F32),
                        pltpu.VMEM((TOP_K,) + _token_shape(tokens), F32),
                        pltpu.SemaphoreType.DMA((2,))],
    )
    return pl.pallas_call(
        functools.partial(_combine_kernel, tokens),
        grid_spec=grid_spec,
        out_shape=jax.ShapeDtypeStruct((n_out, D_MODEL), F32),
        compiler_params=pltpu.CompilerParams(
            dimension_semantics=("arbitrary",), vmem_limit_bytes=VMEM_LIMIT),
        name="combine_ple",
    )(pos, x1, meta, p, ys, *wts)


def _rope_tables(pos, n_rows):
    out = []
    for dim in (ATT_HEAD_DIM, RET_DK):
        half = dim // 2
        inv = ROPE_THETA ** (-np.arange(half, dtype=np.float64) / half)
        ang = np.asarray(pos, np.float64)[:, None] * inv[None, :]
        cos, sin = np.cos(ang), np.sin(ang)
        reps = LANES // dim
        cos_row = np.tile(np.concatenate([cos, cos], axis=-1), (1, reps))
        sin_row = np.tile(np.concatenate([-sin, sin], axis=-1), (1, reps))
        out += [jnp.asarray(np.broadcast_to(cos_row, (n_rows, LANES)), F32),
                jnp.asarray(np.broadcast_to(sin_row, (n_rows, LANES)), F32)]
    return out


def _retention_tables():
    log_g = np.log(1.0 - 2.0 ** (-5.0 - np.arange(RET_HEADS, dtype=np.float64)))
    t = np.arange(BLK, dtype=np.float64)
    rel = t[:, None] - t[None, :]
    intra = np.where(rel >= 0, np.exp(np.maximum(rel, 0.0)[None] * log_g[:, None, None]), 0.0)
    cross = np.exp((t + 1.0)[None, :] * log_g[:, None])
    kdec = np.exp((BLK - 1.0 - t)[None, :] * log_g[:, None])
    lane_bcast = lambda a: np.broadcast_to(a[:, :, None], (RET_HEADS, BLK, LANES))
    tabs = tuple(jnp.asarray(a, F32) for a in (intra, lane_bcast(cross), lane_bcast(kdec)))
    return log_g, tabs


def kernel(x_prompt, x_sample, state_swa_k, state_swa_v, state_ret, p_prompt, p_sample, g_mix, w_in, b_in,
           attn_sinks, g_ret_norm, w_out, b_out, g_mlp, w_router, b_router, w_e_in, b_e_in, w_e_out,
           b_e_out, g_ple, w_ple_gate, w_ple, g_final):
    bsz, seq, _ = x_prompt.shape
    n_dec = x_sample.shape[0]
    n_prompt = bsz * seq
    assert x_sample.shape[1] == 1 and g_mix.shape[0] == 1
    assert state_swa_k.shape[2] == WINDOW and n_dec <= DISPATCH_TOKENS

    row = lambda a: a.reshape(1, -1)
    log_g, ret_tabs = _retention_tables()
    sinks = attn_sinks[0].astype(F32)
    scal_prompt = jnp.concatenate([sinks, jnp.asarray(np.exp(BLK * log_g), F32)])
    scal_sample = jnp.concatenate([sinks, jnp.asarray(np.exp(log_g), F32)])

    w_in_bf = w_in[0].astype(BF16)
    w_out_bf = w_out[0].astype(BF16)
    wr = jnp.pad(w_router[0], ((0, 0), (0, LANES - N_EXPERTS)))
    wr_hi = wr.astype(BF16)
    wr_lo = (wr - wr_hi.astype(F32)).astype(BF16)
    br = jnp.pad(b_router[0], (0, LANES - N_EXPERTS)).reshape(1, LANES)
    post_wts = (row(g_ret_norm[0]), w_out_bf, row(b_out[0]), row(g_mlp[0]), wr_hi, wr_lo, br)

    tabs_p = _rope_tables(np.arange(seq), seq)
    x1_p, h2_p, meta_p, cnt, kp, vp, st_p = _mixer_prompt(
        scal_prompt, x_prompt, tabs_p, ret_tabs, (row(g_mix[0]), w_in_bf, row(b_in[0])) + post_wts)

    tabs_s = _rope_tables(np.full((1,), PAST_LEN), n_dec)
    xs2 = x_sample.reshape(n_dec, D_MODEL)
    aq, ak, av, rq, rk, rv, rg = _sample_inproj(xs2, tabs_s, row(g_mix[0]), w_in_bf, row(b_in[0]))
    att, ret, ks, vs, st_s = _sample_state(
        scal_sample, aq, ak, av, rq, rk, rv,
        state_swa_k[0].reshape(n_dec, WINDOW, LANES), state_swa_v[0].reshape(n_dec, WINDOW, LANES),
        state_ret[0])
    x1_s, h2_s, meta_s, cnt = _sample_post(xs2, att, ret, rg, cnt, post_wts)

    counts = cnt[:, 0].astype(jnp.int32)
    starts = jnp.cumsum(counts) - counts
    def sorted_rows(meta):
        sel = meta[:, META_IDX:META_IDX + TOP_K].astype(jnp.int32)
        rank = meta[:, META_RANK:META_RANK + TOP_K].astype(jnp.int32)
        is_e = sel[:, :, None] == jnp.arange(N_EXPERTS, dtype=jnp.int32)[None, None, :]
        return (jnp.sum(jnp.where(is_e, starts[None, None, :], 0), axis=-1) + rank).reshape(-1)

    pos_p, pos_s = sorted_rows(meta_p), sorted_rows(meta_s)

    xs_sorted = _dispatch(pos_p, pos_s, h2_p, h2_s)
    visits = _visit_schedule(counts, (n_prompt + n_dec) * TOP_K)
    ys_sorted = _experts(visits, xs_sorted, w_e_in[0], b_e_in[0], w_e_out[0], b_e_out[0])

    ple_wts = (row(g_ple[0]), w_ple_gate[0].astype(BF16), w_ple[0].astype(BF16), row(g_final))
    y_p = _combine(COMBINE_TOKENS, pos_p, x1_p, meta_p, p_prompt[0].reshape(n_prompt, PLE_DIM),
                   ys_sorted, ple_wts)
    y_s = _combine(n_dec, pos_s, x1_s, meta_s, p_sample[0].reshape(n_dec, PLE_DIM), ys_sorted, ple_wts)

    kv_shape = (1, bsz, WINDOW, ATT_KV_HEADS, ATT_HEAD_DIM)
    dec_shape = (1, n_dec, WINDOW, ATT_KV_HEADS, ATT_HEAD_DIM)
    return (y_p.reshape(bsz, seq, D_MODEL), y_s.reshape(n_dec, 1, D_MODEL),
            kp.reshape(kv_shape), vp.reshape(kv_shape), st_p[None],
            ks.reshape(dec_shape), vs.reshape(dec_shape), st_s[None])
```
